```python
import math
import jax, jax.numpy as jnp
from jax import lax
import numpy as np

D_MODEL = 2048
BATCH = 4
SEQ = 4096
DEPTH = 1

CHUNK = 64
D_FF = 5632
P_DIM = 256
SGU_BLOCK = 128
SGU_GROUPS = 8
SGU_GROUP_DIM = 128
SGU_WIDTH = SGU_GROUPS * SGU_GROUP_DIM
N_HEADS = 8
HEAD_DIM = 64
V_HEAD_DIM = 2 * HEAD_DIM
QK_WIDTH = N_HEADS * 2 * HEAD_DIM
ATTN_WIDTH = N_HEADS * V_HEAD_DIM
Q_BLOCK = 128
IN_COLS = 2 * SGU_WIDTH + 2 * QK_WIDTH + ATTN_WIDTH + 2 * D_MODEL
ALPHA = (2 * DEPTH) ** 0.25
BETA = (8 * DEPTH) ** -0.25
LN_EPS = 1e-5

kernel_name = "hybrid_gmlp_diffattn_macaron_deepnorm"


def layer_norm(x, g, b):
    xf = x.astype(jnp.float32)
    mu = jnp.mean(xf, axis=-1, keepdims=True)
    var = jnp.mean(jnp.square(xf - mu), axis=-1, keepdims=True)
    y = (xf - mu) * lax.rsqrt(var + LN_EPS)
    return (y * g.astype(jnp.float32) + b.astype(jnp.float32)).astype(x.dtype)


def swiglu(x, w_gu, w_down):
    gate, up = jnp.split(x @ w_gu, 2, axis=-1)
    return (jax.nn.silu(gate) * up) @ w_down


def gmlp_sgu(u, v, ln_g, ln_b, w_s, b_s):
    B, S, _ = u.shape
    v = layer_norm(v, ln_g, ln_b)
    pos = jnp.arange(SGU_BLOCK)
    allowed = (pos[None, :] // CHUNK) <= (pos[:, None] // CHUNK)
    w = jnp.where(allowed[None], w_s, jnp.zeros_like(w_s))
    vb = v.reshape(B, S // SGU_BLOCK, SGU_BLOCK, SGU_GROUPS, SGU_GROUP_DIM)
    s = jnp.einsum('gts,bnsgc->bntgc', w, vb) + b_s.T[None, None, :, :, None]
    return u * s.reshape(B, S, SGU_WIDTH)


def diff_attention(q, k, v, lam, slopes):
    B, S, H = q.shape[0], q.shape[1], q.shape[2]
    nq = S // Q_BLOCK
    scale = HEAD_DIM ** -0.5
    qb = q.reshape(B, nq, Q_BLOCK, H, 2, HEAD_DIM).transpose(1, 0, 2, 3, 4, 5)
    kpos = jnp.arange(S)

    def block(args):
        qi, qblk = args
        tpos = qi * Q_BLOCK + jnp.arange(Q_BLOCK)
        s = jnp.einsum('bqhmd,bkhmd->bhmqk', qblk, k).astype(jnp.float32) * scale
        dist = jnp.abs(tpos[:, None] - kpos[None, :]).astype(jnp.float32)
        allowed = (kpos[None, :] // CHUNK) <= (tpos[:, None] // CHUNK)
        s = s - slopes[None, :, None, None, None] * dist
        s = jnp.where(allowed, s, -jnp.inf)
        probs = jax.nn.softmax(s, axis=-1)
        a = probs[:, :, 0] - lam * probs[:, :, 1]
        return jnp.einsum('bhqk,bkhe->bqhe', a.astype(v.dtype), v)

    out = lax.map(block, (jnp.arange(nq), qb))
    return out.transpose(1, 0, 2, 3, 4).reshape(B, S, H, V_HEAD_DIM)


def head_rms_norm(o, g):
    of = o.astype(jnp.float32)
    y = of * lax.rsqrt(jnp.mean(jnp.square(of), axis=-1, keepdims=True) + LN_EPS)
    return (y * g.reshape(N_HEADS, V_HEAD_DIM).astype(jnp.float32)).astype(o.dtype)


def setup_inputs(seed: int = 0) -> dict:
    key = jax.random.key(seed)
    ks = iter(jax.random.split(key, 40))
    f32 = jnp.float32
    L = DEPTH

    def nrm(shape, scale):
        return jax.random.normal(next(ks), shape, f32) * scale

    def gain(shape):
        return 1.0 + nrm(shape, 0.05)

    def bias(shape):
        return nrm(shape, 0.02)

    return {
        "x": nrm((BATCH, SEQ, D_MODEL), 1.0),
        "p": nrm((DEPTH, BATCH, SEQ, P_DIM), 1.0),
        "ffn1_w_gu": nrm((L, D_MODEL, 2 * D_FF), D_MODEL ** -0.5),
        "ffn1_w_down": nrm((L, D_FF, D_MODEL), BETA * D_FF ** -0.5),
        "ln1_g": gain((L, D_MODEL)),
        "ln1_b": bias((L, D_MODEL)),
        "w_in": nrm((L, D_MODEL, IN_COLS), D_MODEL ** -0.5),
        "sgu_ln_g": gain((L, SGU_WIDTH)),
        "sgu_ln_b": bias((L, SGU_WIDTH)),
        "sgu_w": nrm((L, SGU_GROUPS, SGU_BLOCK, SGU_BLOCK), SGU_BLOCK ** -0.5),
        "sgu_b": 1.0 + nrm((L, SGU_GROUPS, SGU_BLOCK), 0.1),
        "lam_q1": nrm((L, HEAD_DIM), 0.1),
        "lam_k1": nrm((L, HEAD_DIM), 0.1),
        "lam_q2": nrm((L, HEAD_DIM), 0.1),
        "lam_k2": nrm((L, HEAD_DIM), 0.1),
        "attn_norm_g": gain((L, ATTN_WIDTH)),
        "w_branch_a": nrm((L, SGU_WIDTH, D_MODEL), SGU_WIDTH ** -0.5),
        "w_branch_b": nrm((L, ATTN_WIDTH, D_MODEL), ATTN_WIDTH ** -0.5),
        "w_out": nrm((L, D_MODEL, D_MODEL), BETA * D_MODEL ** -0.5),
        "ln2_g": gain((L, D_MODEL)),
        "ln2_b": bias((L, D_MODEL)),
        "ffn2_w_gu": nrm((L, D_MODEL, 2 * D_FF), D_MODEL ** -0.5),
        "ffn2_w_down": nrm((L, D_FF, D_MODEL), BETA * D_FF ** -0.5),
        "ln3_g": gain((L, D_MODEL)),
        "ln3_b": bias((L, D_MODEL)),
        "w_pe_gate": nrm((L, D_MODEL, D_MODEL), D_MODEL ** -0.5),
        "w_pe_proj": nrm((L, P_DIM, D_MODEL), BETA * P_DIM ** -0.5),
        "ln4_g": gain((L, D_MODEL)),
        "ln4_b": bias((L, D_MODEL)),
    }


def reference(x, p, ffn1_w_gu, ffn1_w_down, ln1_g, ln1_b, w_in, sgu_ln_g, sgu_ln_b,
              sgu_w, sgu_b, lam_q1, lam_k1, lam_q2, lam_k2, attn_norm_g, w_branch_a,
              w_branch_b, w_out, ln2_g, ln2_b, ffn2_w_gu, ffn2_w_down, ln3_g, ln3_b,
              w_pe_gate, w_pe_proj, ln4_g, ln4_b):
    B, S, _ = x.shape
    slopes = jnp.asarray(2.0 ** (-8.0 * np.arange(1, N_HEADS + 1) / N_HEADS), dtype=jnp.float32)
    splits = np.cumsum([SGU_WIDTH, SGU_WIDTH, QK_WIDTH, QK_WIDTH, ATTN_WIDTH, D_MODEL]).tolist()
    for i in range(DEPTH):
        lam_init = 0.8 - 0.6 * math.exp(-0.3 * i)
        x = layer_norm(ALPHA * x + 0.5 * swiglu(x, ffn1_w_gu[i], ffn1_w_down[i]), ln1_g[i], ln1_b[i])
        proj = x @ w_in[i]
        u, v, q, k, val, g_a, g_b = jnp.split(proj, splits, axis=-1)
        y_a = gmlp_sgu(jax.nn.gelu(u), jax.nn.gelu(v), sgu_ln_g[i], sgu_ln_b[i], sgu_w[i], sgu_b[i])
        lam = (jnp.exp(jnp.sum(lam_q1[i].astype(jnp.float32) * lam_k1[i].astype(jnp.float32)))
               - jnp.exp(jnp.sum(lam_q2[i].astype(jnp.float32) * lam_k2[i].astype(jnp.float32)))
               + lam_init)
        o = diff_attention(q.reshape(B, S, N_HEADS, 2, HEAD_DIM),
                           k.reshape(B, S, N_HEADS, 2, HEAD_DIM),
                           val.reshape(B, S, N_HEADS, V_HEAD_DIM), lam, slopes)
        y_b = (head_rms_norm(o, attn_norm_g[i]) * (1.0 - lam_init)).reshape(B, S, ATTN_WIDTH)
        merged = jax.nn.sigmoid(g_a) * (y_a @ w_branch_a[i]) + jax.nn.sigmoid(g_b) * (y_b @ w_branch_b[i])
        x = layer_norm(ALPHA * x + merged @ w_out[i], ln2_g[i], ln2_b[i])
        x = layer_norm(ALPHA * x + 0.5 * swiglu(x, ffn2_w_gu[i], ffn2_w_down[i]), ln3_g[i], ln3_b[i])
        e = jax.nn.sigmoid(x @ w_pe_gate[i]) * (p[i] @ w_pe_proj[i])
        x = layer_norm(ALPHA * x + e, ln4_g[i], ln4_b[i])
    return x
```

```python
import functools
import math

import jax
import jax.numpy as jnp
import numpy as np
from jax import lax
from jax.experimental import pallas as pl
from jax.experimental.pallas import tpu as pltpu

D_MODEL = 2048
CHUNK = 64
D_FF = 5632
P_DIM = 256
SGU_BLOCK = 128
SGU_GROUPS = 8
SGU_WIDTH = 1024
N_HEADS = 8
HEAD_DIM = 64
V_HEAD_DIM = 128
QK_WIDTH = 1024
ATTN_WIDTH = 1024
DEPTH = 1
ALPHA = (2 * DEPTH) ** 0.25
LN_EPS = 1e-5
LAM_INIT = 0.8 - 0.6 * math.exp(-0.3 * 0)

COL_U, COL_V, COL_Q, COL_K, COL_VAL = 0, 1024, 2048, 3072, 4096
COL_GA, COL_GB = 5120, 7168
MIX_COLS = 5120

BF16 = jnp.bfloat16
F32 = jnp.float32

VMEM_LIMIT_BYTES = 56 * 1024 * 1024

MASK_VALUE = -1e30


def _params(semantics):
    return pltpu.CompilerParams(dimension_semantics=semantics, vmem_limit_bytes=VMEM_LIMIT_BYTES)


def _dot(a, b):
    return jnp.dot(a, b, preferred_element_type=F32)


def _layer_norm(y, g, b):
    mu = jnp.mean(y, axis=-1, keepdims=True)
    d = y - mu
    var = jnp.mean(d * d, axis=-1, keepdims=True)
    return d * lax.rsqrt(var + LN_EPS) * g + b


FFN_TM = 512
FFN_TF = 512


def _ffn_ln_kernel(x_ref, wg_ref, wu_ref, wd_ref, g_ref, b_ref, of_ref, ob_ref, xb_scr, acc_scr):
    j = pl.program_id(1)

    @pl.when(j == 0)
    def _():
        xb_scr[...] = x_ref[...].astype(BF16)
        acc_scr[...] = jnp.zeros_like(acc_scr)

    xb = xb_scr[...]
    gate = _dot(xb, wg_ref[...])
    up = _dot(xb, wu_ref[...])
    h = (gate * jax.nn.sigmoid(gate)) * up
    acc_scr[...] += _dot(h.astype(BF16), wd_ref[...])

    @pl.when(j == pl.num_programs(1) - 1)
    def _():
        y = ALPHA * x_ref[...] + 0.5 * acc_scr[...]
        out = _layer_norm(y, g_ref[...], b_ref[...])
        of_ref[...] = out
        ob_ref[...] = out.astype(BF16)


def _ffn_ln(x, w_gu, w_down, g, b):
    m, d = x.shape
    f = w_down.shape[0]
    nf = f // FFN_TF
    return pl.pallas_call(
        _ffn_ln_kernel,
        grid=(m // FFN_TM, nf),
        in_specs=[
            pl.BlockSpec((FFN_TM, d), lambda i, j: (i, 0)),
            pl.BlockSpec((d, FFN_TF), lambda i, j: (0, j)),
            pl.BlockSpec((d, FFN_TF), lambda i, j: (0, nf + j)),
            pl.BlockSpec((FFN_TF, d), lambda i, j: (j, 0)),
            pl.BlockSpec((1, d), lambda i, j: (0, 0)),
            pl.BlockSpec((1, d), lambda i, j: (0, 0)),
        ],
        out_specs=[
            pl.BlockSpec((FFN_TM, d), lambda i, j: (i, 0)),
            pl.BlockSpec((FFN_TM, d), lambda i, j: (i, 0)),
        ],
        out_shape=[jax.ShapeDtypeStruct((m, d), F32), jax.ShapeDtypeStruct((m, d), BF16)],
        scratch_shapes=[pltpu.VMEM((FFN_TM, d), BF16), pltpu.VMEM((FFN_TM, d), F32)],
        compiler_params=_params(("parallel", "arbitrary")),
        name="ffn_ln",
    )(x, w_gu, w_gu, w_down, g, b)


PROJ_TM = 512
PROJ_TN = 1024


def _in_proj_kernel(x_ref, w_ref, g_ref, b_ref, o_ref):
    j = pl.program_id(1)
    acc = _dot(x_ref[...], w_ref[...])

    @pl.when(j == COL_U // PROJ_TN)
    def _():
        o_ref[...] = jax.nn.gelu(acc).astype(BF16)

    @pl.when(j == COL_V // PROJ_TN)
    def _():
        o_ref[...] = _layer_norm(jax.nn.gelu(acc), g_ref[...], b_ref[...]).astype(BF16)

    @pl.when(j == COL_Q // PROJ_TN)
    def _():
        o_ref[...] = (acc * HEAD_DIM ** -0.5).astype(BF16)

    @pl.when(j >= COL_K // PROJ_TN)
    def _():
        o_ref[...] = acc.astype(BF16)


def _in_proj(xb, w_in, sgu_g, sgu_b):
    m, d = xb.shape
    return pl.pallas_call(
        _in_proj_kernel,
        grid=(m // PROJ_TM, MIX_COLS // PROJ_TN),
        in_specs=[
            pl.BlockSpec((PROJ_TM, d), lambda i, j: (i, 0)),
            pl.BlockSpec((d, PROJ_TN), lambda i, j: (0, j)),
            pl.BlockSpec((1, SGU_WIDTH), lambda i, j: (0, 0)),
            pl.BlockSpec((1, SGU_WIDTH), lambda i, j: (0, 0)),
        ],
        out_specs=pl.BlockSpec((PROJ_TM, PROJ_TN), lambda i, j: (i, j)),
        out_shape=jax.ShapeDtypeStruct((m, MIX_COLS), BF16),
        compiler_params=_params(("parallel", "arbitrary")),
        name="in_proj",
    )(xb, w_in, sgu_g, sgu_b)


SGU_TM = 512


def _sgu_kernel(u_ref, v_ref, w_ref, bias_ref, o_ref):
    t = lax.broadcasted_iota(jnp.int32, (SGU_BLOCK, SGU_BLOCK), 0)
    s = lax.broadcasted_iota(jnp.int32, (SGU_BLOCK, SGU_BLOCK), 1)
    allowed = (s // CHUNK) <= (t // CHUNK)
    for g in range(SGU_GROUPS):
        wm = jnp.where(allowed, w_ref[g], 0.0).astype(BF16)
        cols = slice(g * SGU_BLOCK, (g + 1) * SGU_BLOCK)
        bias = bias_ref[:, cols]
        for blk in range(SGU_TM // SGU_BLOCK):
            rows = slice(blk * SGU_BLOCK, (blk + 1) * SGU_BLOCK)
            mix = _dot(wm, v_ref[rows, cols]) + bias
            o_ref[rows, cols] = (u_ref[rows, cols].astype(F32) * mix).astype(BF16)


def _sgu(mix, sgu_w, bias_full):
    m = mix.shape[0]
    return pl.pallas_call(
        _sgu_kernel,
        grid=(m // SGU_TM,),
        in_specs=[
            pl.BlockSpec((SGU_TM, SGU_WIDTH), lambda i: (i, COL_U // SGU_WIDTH)),
            pl.BlockSpec((SGU_TM, SGU_WIDTH), lambda i: (i, COL_V // SGU_WIDTH)),
            pl.BlockSpec((SGU_GROUPS, SGU_BLOCK, SGU_BLOCK), lambda i: (0, 0, 0)),
            pl.BlockSpec((SGU_BLOCK, SGU_WIDTH), lambda i: (0, 0)),
        ],
        out_specs=pl.BlockSpec((SGU_TM, SGU_WIDTH), lambda i: (i, 0)),
        out_shape=jax.ShapeDtypeStruct((m, SGU_WIDTH), BF16),
        compiler_params=_params(("parallel",)),
        name="sgu",
    )(mix, mix, sgu_w, bias_full)


ATT_TQ = 256
ATT_TK = 256


def _attn_kernel(slopes_ref, lam_ref, q_ref, k_ref, v_ref, g_ref, o_ref, m_scr, l_scr, acc_scr):
    h = pl.program_id(1)
    qi = pl.program_id(2)
    tq, tk = ATT_TQ, ATT_TK
    slope = slopes_ref[h]

    q = q_ref[...]
    lane = lax.broadcasted_iota(jnp.int32, q.shape, 1)
    zero = jnp.zeros_like(q)
    qs = jnp.concatenate([jnp.where(lane < HEAD_DIM, q, zero), jnp.where(lane >= HEAD_DIM, q, zero)], axis=0)

    row = lax.broadcasted_iota(jnp.int32, (2 * tq, tk), 0)
    col = lax.broadcasted_iota(jnp.int32, (2 * tq, tk), 1)
    trow = jnp.where(row >= tq, row - tq, row)
    rel = (trow - col).astype(F32)

    def scores(kt):
        return lax.dot_general(qs, kt, (((1,), (1,)), ((), ())), preferred_element_type=F32)

    q0 = pl.multiple_of(qi * tq, tq)
    s = scores(k_ref[pl.ds(q0, tk), :]) - slope * jnp.abs(rel)
    s = jnp.where((col // CHUNK) <= (trow // CHUNK), s, MASK_VALUE)
    m0 = jnp.max(s, axis=1, keepdims=True)
    p = jnp.exp(s - m0)
    m_scr[...] = m0
    l_scr[...] = jnp.sum(p, axis=1, keepdims=True)
    acc_scr[...] = _dot(p.astype(BF16), v_ref[pl.ds(q0, tk), :])

    def body(kb, carry):
        k0 = pl.multiple_of(kb * tk, tk)
        off = (q0 - k0).astype(F32)
        s = scores(k_ref[pl.ds(k0, tk), :]) - slope * (rel + off)
        m_old = m_scr[...]
        m_new = jnp.maximum(m_old, jnp.max(s, axis=1, keepdims=True))
        a = jnp.exp(m_old - m_new)
        p = jnp.exp(s - m_new)
        m_scr[...] = m_new
        l_scr[...] = a * l_scr[...] + jnp.sum(p, axis=1, keepdims=True)
        acc_scr[...] = a * acc_scr[...] + _dot(p.astype(BF16), v_ref[pl.ds(k0, tk), :])
        return carry

    lax.fori_loop(0, qi, body, 0)

    lv = lam_ref[...]
    lam = (jnp.exp(jnp.sum(lv[0:1] * lv[1:2], axis=-1, keepdims=True))
           - jnp.exp(jnp.sum(lv[2:3] * lv[3:4], axis=-1, keepdims=True)) + LAM_INIT)
    o_n = acc_scr[...] / l_scr[...]
    o = o_n[:tq] - lam * o_n[tq:]
    y = o * lax.rsqrt(jnp.mean(o * o, axis=-1, keepdims=True) + LN_EPS)
    o_ref[...] = (y * g_ref[...] * (1.0 - LAM_INIT)).astype(BF16)


def _diff_attn(mix, slopes, lam_vecs, attn_g, batch, seq):
    m = mix.shape[0]
    nq = seq // ATT_TQ
    qc, kc, vc = COL_Q // V_HEAD_DIM, COL_K // V_HEAD_DIM, COL_VAL // V_HEAD_DIM
    return pl.pallas_call(
        _attn_kernel,
        grid=(batch, N_HEADS, nq),
        in_specs=[
            pl.BlockSpec(memory_space=pltpu.SMEM),
            pl.BlockSpec((4, HEAD_DIM), lambda b, h, i: (0, 0)),
            pl.BlockSpec((ATT_TQ, V_HEAD_DIM), lambda b, h, i: (b * nq + i, qc + h)),
            pl.BlockSpec((seq, V_HEAD_DIM), lambda b, h, i: (b, kc + h)),
            pl.BlockSpec((seq, V_HEAD_DIM), lambda b, h, i: (b, vc + h)),
            pl.BlockSpec((1, V_HEAD_DIM), lambda b, h, i: (0, h)),
        ],
        out_specs=pl.BlockSpec((ATT_TQ, V_HEAD_DIM), lambda b, h, i: (b * nq + i, h)),
        out_shape=jax.ShapeDtypeStruct((m, ATTN_WIDTH), BF16),
        scratch_shapes=[
            pltpu.VMEM((2 * ATT_TQ, 1), F32),
            pltpu.VMEM((2 * ATT_TQ, 1), F32),
            pltpu.VMEM((2 * ATT_TQ, V_HEAD_DIM), F32),
        ],
        compiler_params=_params(("parallel", "parallel", "arbitrary")),
        name="diff_attn",
    )(slopes, lam_vecs, mix, mix, mix, attn_g)


MERGE_TM = 512
MERGE_TC = 512


def _merge_kernel(xb_ref, xf_ref, ya_ref, yb_ref, wga_ref, wgb_ref, wa_ref, wb_ref, wo_ref, g_ref, b_ref,
                  o_ref, acc_scr):
    j = pl.program_id(1)

    @pl.when(j == 0)
    def _():
        acc_scr[...] = jnp.zeros_like(acc_scr)

    xb = xb_ref[...]
    merged = (jax.nn.sigmoid(_dot(xb, wga_ref[...])) * _dot(ya_ref[...], wa_ref[...])
              + jax.nn.sigmoid(_dot(xb, wgb_ref[...])) * _dot(yb_ref[...], wb_ref[...]))
    acc_scr[...] += _dot(merged.astype(BF16), wo_ref[...])

    @pl.when(j == pl.num_programs(1) - 1)
    def _():
        o_ref[...] = _layer_norm(ALPHA * xf_ref[...] + acc_scr[...], g_ref[...], b_ref[...])


def _merge_out(xb, xf, ya, yb, w_in, w_a, w_b, w_o, g, b):
    m, d = xf.shape
    nc = d // MERGE_TC
    ga0, gb0 = COL_GA // MERGE_TC, COL_GB // MERGE_TC
    row = lambda i, j: (i, 0)
    return pl.pallas_call(
        _merge_kernel,
        grid=(m // MERGE_TM, nc),
        in_specs=[
            pl.BlockSpec((MERGE_TM, d), row),
            pl.BlockSpec((MERGE_TM, d), row),
            pl.BlockSpec((MERGE_TM, SGU_WIDTH), row),
            pl.BlockSpec((MERGE_TM, ATTN_WIDTH), row),
            pl.BlockSpec((d, MERGE_TC), lambda i, j: (0, ga0 + j)),
            pl.BlockSpec((d, MERGE_TC), lambda i, j: (0, gb0 + j)),
            pl.BlockSpec((SGU_WIDTH, MERGE_TC), lambda i, j: (0, j)),
            pl.BlockSpec((ATTN_WIDTH, MERGE_TC), lambda i, j: (0, j)),
            pl.BlockSpec((MERGE_TC, d), lambda i, j: (j, 0)),
            pl.BlockSpec((1, d), lambda i, j: (0, 0)),
            pl.BlockSpec((1, d), lambda i, j: (0, 0)),
        ],
        out_specs=pl.BlockSpec((MERGE_TM, d), row),
        out_shape=jax.ShapeDtypeStruct((m, d), F32),
        scratch_shapes=[pltpu.VMEM((MERGE_TM, d), F32)],
        compiler_params=_params(("parallel", "arbitrary")),
        name="merge_out",
    )(xb, xf, ya, yb, w_in, w_in, w_a, w_b, w_o, g, b)


PE_TM = 512
PE_TN = 512


def _pe_kernel(xb_ref, xf_ref, p_ref, wg_ref, wp_ref, g_ref, b_ref, o_ref, y_scr):
    j = pl.program_id(1)
    gate = jax.nn.sigmoid(_dot(xb_ref[...], wg_ref[...]))
    emb = _dot(p_ref[...].astype(BF16), wp_ref[...])
    c0 = pl.multiple_of(j * PE_TN, PE_TN)
    y_scr[:, pl.ds(c0, PE_TN)] = ALPHA * xf_ref[:, pl.ds(c0, PE_TN)] + gate * emb

    @pl.when(j == pl.num_programs(1) - 1)
    def _():
        o_ref[...] = _layer_norm(y_scr[...], g_ref[...], b_ref[...])


def _pe_gate(xb, xf, p, w_g, w_p, g, b):
    m, d = xf.shape
    row = lambda i, j: (i, 0)
    return pl.pallas_call(
        _pe_kernel,
        grid=(m // PE_TM, d // PE_TN),
        in_specs=[
            pl.BlockSpec((PE_TM, d), row),
            pl.BlockSpec((PE_TM, d), row),
            pl.BlockSpec((PE_TM, P_DIM), row),
            pl.BlockSpec((d, PE_TN), lambda i, j: (0, j)),
            pl.BlockSpec((P_DIM, PE_TN), lambda i, j: (0, j)),
            pl.BlockSpec((1, d), lambda i, j: (0, 0)),
            pl.BlockSpec((1, d), lambda i, j: (0, 0)),
        ],
        out_specs=pl.BlockSpec((PE_TM, d), row),
        out_shape=jax.ShapeDtypeStruct((m, d), F32),
        scratch_shapes=[pltpu.VMEM((PE_TM, d), F32)],
        compiler_params=_params(("parallel", "arbitrary")),
        name="pe_gate",
    )(xb, xf, p, w_g, w_p, g, b)


def kernel(x, p, ffn1_w_gu, ffn1_w_down, ln1_g, ln1_b, w_in, sgu_ln_g, sgu_ln_b, sgu_w, sgu_b, lam_q1, lam_k1, lam_q2, lam_k2, attn_norm_g, w_branch_a, w_branch_b, w_out, ln2_g, ln2_b, ffn2_w_gu, ffn2_w_down, ln3_g, ln3_b, w_pe_gate, w_pe_proj, ln4_g, ln4_b):
    batch, seq, d = x.shape
    m = batch * seq
    slopes = jnp.asarray(2.0 ** (-8.0 * np.arange(1, N_HEADS + 1) / N_HEADS), dtype=F32)
    xf = x.reshape(m, d)
    for i in range(DEPTH):
        w_in_b = w_in[i].astype(BF16)
        bias_full = jnp.repeat(sgu_b[i].T, SGU_BLOCK, axis=1)
        lam_vecs = jnp.stack([lam_q1[i], lam_k1[i], lam_q2[i], lam_k2[i]]).astype(F32)

        x1f, x1b = _ffn_ln(xf, ffn1_w_gu[i].astype(BF16), ffn1_w_down[i].astype(BF16),
                           ln1_g[i][None], ln1_b[i][None])
        mix = _in_proj(x1b, w_in_b, sgu_ln_g[i][None], sgu_ln_b[i][None])
        y_a = _sgu(mix, sgu_w[i], bias_full)
        y_b = _diff_attn(mix, slopes, lam_vecs, attn_norm_g[i][None], batch, seq)
        x2f = _merge_out(x1b, x1f, y_a, y_b, w_in_b, w_branch_a[i].astype(BF16), w_branch_b[i].astype(BF16),
                            w_out[i].astype(BF16), ln2_g[i][None], ln2_b[i][None])
        x3f, x3b = _ffn_ln(x2f, ffn2_w_gu[i].astype(BF16), ffn2_w_down[i].astype(BF16),
                           ln3_g[i][None], ln3_b[i][None])
        xf = _pe_gate(x3b, x3f, p[i].reshape(m, P_DIM), w_pe_gate[i].astype(BF16), w_pe_proj[i].astype(BF16),
                      ln4_g[i][None], ln4_b[i][None])
    return xf.reshape(batch, seq, d)
```

```python
import functools
import math

import jax
import jax.numpy as jnp
import numpy as np
from jax import lax
from jax.experimental import pallas as pl
from jax.experimental.pallas import tpu as pltpu

D_MODEL = 2048
CHUNK = 64
D_FF = 5632
P_DIM = 256
SGU_BLOCK = 128
SGU_GROUPS = 8
SGU_WIDTH = 1024
N_HEADS = 8
HEAD_DIM = 64
V_HEAD_DIM = 128
QK_WIDTH = 1024
ATTN_WIDTH = 1024
DEPTH = 1
ALPHA = (2 * DEPTH) ** 0.25
LN_EPS = 1e-5
LAM_INIT = 0.8 - 0.6 * math.exp(-0.3 * 0)

COL_U, COL_V, COL_Q, COL_K, COL_VAL = 0, 1024, 2048, 3072, 4096
COL_GA, COL_GB = 5120, 7168
MIX_COLS = 5120

BF16 = jnp.bfloat16
F32 = jnp.float32

VMEM_LIMIT_BYTES = 56 * 1024 * 1024

MASK_VALUE = -1e30


def _params(semantics):
    return pltpu.CompilerParams(dimension_semantics=semantics, vmem_limit_bytes=VMEM_LIMIT_BYTES)


def _dot(a, b):
    return jnp.dot(a, b, preferred_element_type=F32)


def _layer_norm(y, g, b):
    mu = jnp.mean(y, axis=-1, keepdims=True)
    d = y - mu
    var = jnp.mean(d * d, axis=-1, keepdims=True)
    return d * lax.rsqrt(var + LN_EPS) * g + b


FFN_TM = 512
FFN_TF = 512


def _ffn_ln_kernel(x_ref, wg_ref, wu_ref, wd_ref, g_ref, b_ref, of_ref, ob_ref, xb_scr, acc_scr):
    j = pl.program_id(1)

    @pl.when(j == 0)
    def _():
        xb_scr[...] = x_ref[...].astype(BF16)
        acc_scr[...] = jnp.zeros_like(acc_scr)

    xb = xb_scr[...]
    gate = _dot(xb, wg_ref[...])
    up = _dot(xb, wu_ref[...])
    h = (gate * jax.nn.sigmoid(gate)) * up
    acc_scr[...] += _dot(h.astype(BF16), wd_ref[...])

    @pl.when(j == pl.num_programs(1) - 1)
    def _():
        y = ALPHA * x_ref[...] + 0.5 * acc_scr[...]
        out = _layer_norm(y, g_ref[...], b_ref[...])
        of_ref[...] = out
        ob_ref[...] = out.astype(BF16)


def _ffn_ln(x, w_gu, w_down, g, b):
    m, d = x.shape
    f = w_down.shape[0]
    nf = f // FFN_TF
    return pl.pallas_call(
        _ffn_ln_kernel,
        grid=(m // FFN_TM, nf),
        in_specs=[
            pl.BlockSpec((FFN_TM, d), lambda i, j: (i, 0)),
            pl.BlockSpec((d, FFN_TF), lambda i, j: (0, j)),
            pl.BlockSpec((d, FFN_TF), lambda i, j: (0, nf + j)),
            pl.BlockSpec((FFN_TF, d), lambda i, j: (j, 0)),
            pl.BlockSpec((1, d), lambda i, j: (0, 0)),
            pl.BlockSpec((1, d), lambda i, j: (0, 0)),
        ],
        out_specs=[
            pl.BlockSpec((FFN_TM, d), lambda i, j: (i, 0)),
            pl.BlockSpec((FFN_TM, d), lambda i, j: (i, 0)),
        ],
        out_shape=[jax.ShapeDtypeStruct((m, d), F32), jax.ShapeDtypeStruct((m, d), BF16)],
        scratch_shapes=[pltpu.VMEM((FFN_TM, d), BF16), pltpu.VMEM((FFN_TM, d), F32)],
        compiler_params=_params(("parallel", "arbitrary")),
        name="ffn_ln",
    )(x, w_gu, w_gu, w_down, g, b)


PROJ_TM = 512
PROJ_TN = 1024


def _in_proj_kernel(x_ref, w_ref, g_ref, b_ref, o_ref):
    j = pl.program_id(1)
    acc = _dot(x_ref[...], w_ref[...])

    @pl.when(j == COL_U // PROJ_TN)
    def _():
        o_ref[...] = jax.nn.gelu(acc).astype(BF16)

    @pl.when(j == COL_V // PROJ_TN)
    def _():
        o_ref[...] = _layer_norm(jax.nn.gelu(acc), g_ref[...], b_ref[...]).astype(BF16)

    @pl.when(j == COL_Q // PROJ_TN)
    def _():
        o_ref[...] = (acc * HEAD_DIM ** -0.5).astype(BF16)

    @pl.when(j >= COL_K // PROJ_TN)
    def _():
        o_ref[...] = acc.astype(BF16)


def _in_proj(xb, w_in, sgu_g, sgu_b):
    m, d = xb.shape
    return pl.pallas_call(
        _in_proj_kernel,
        grid=(m // PROJ_TM, MIX_COLS // PROJ_TN),
        in_specs=[
            pl.BlockSpec((PROJ_TM, d), lambda i, j: (i, 0)),
            pl.BlockSpec((d, PROJ_TN), lambda i, j: (0, j)),
            pl.BlockSpec((1, SGU_WIDTH), lambda i, j: (0, 0)),
            pl.BlockSpec((1, SGU_WIDTH), lambda i, j: (0, 0)),
        ],
        out_specs=pl.BlockSpec((PROJ_TM, PROJ_TN), lambda i, j: (i, j)),
        out_shape=jax.ShapeDtypeStruct((m, MIX_COLS), BF16),
        compiler_params=_params(("parallel", "arbitrary")),
        name="in_proj",
    )(xb, w_in, sgu_g, sgu_b)


SGU_TM = 512


def _sgu_kernel(u_ref, v_ref, w_ref, bias_ref, o_ref):
    t = lax.broadcasted_iota(jnp.int32, (SGU_BLOCK, SGU_BLOCK), 0)
    s = lax.broadcasted_iota(jnp.int32, (SGU_BLOCK, SGU_BLOCK), 1)
    allowed = (s // CHUNK) <= (t // CHUNK)
    for g in range(SGU_GROUPS):
        wm = jnp.where(allowed, w_ref[g], 0.0).astype(BF16)
        cols = slice(g * SGU_BLOCK, (g + 1) * SGU_BLOCK)
        bias = bias_ref[:, cols]
        for blk in range(SGU_TM // SGU_BLOCK):
            rows = slice(blk * SGU_BLOCK, (blk + 1) * SGU_BLOCK)
            mix = _dot(wm, v_ref[rows, cols]) + bias
            o_ref[rows, cols] = (u_ref[rows, cols].astype(F32) * mix).astype(BF16)


def _sgu(mix, sgu_w, bias_full):
    m = mix.shape[0]
    return pl.pallas_call(
        _sgu_kernel,
        grid=(m // SGU_TM,),
        in_specs=[
            pl.BlockSpec((SGU_TM, SGU_WIDTH), lambda i: (i, COL_U // SGU_WIDTH)),
            pl.BlockSpec((SGU_TM, SGU_WIDTH), lambda i: (i, COL_V // SGU_WIDTH)),
            pl.BlockSpec((SGU_GROUPS, SGU_BLOCK, SGU_BLOCK), lambda i: (0, 0, 0)),
            pl.BlockSpec((SGU_BLOCK, SGU_WIDTH), lambda i: (0, 0)),
        ],
        out_specs=pl.BlockSpec((SGU_TM, SGU_WIDTH), lambda i: (i, 0)),
        out_shape=jax.ShapeDtypeStruct((m, SGU_WIDTH), BF16),
        compiler_params=_params(("parallel",)),
        name="sgu",
    )(mix, mix, sgu_w, bias_full)


ATT_TK = 256
ATT_TQ = 2 * ATT_TK
ATT_ROWS = 2 * ATT_TQ
AUG = 128


def _attn_kernel(slopes_ref, lam_ref, q_ref, k_ref, v_ref, g_ref, o_ref,
                 kaug_scr, vaug_scr, qa_scr, sa_scr, sb_scr, m_scr, acc_scr):
    h = pl.program_id(1)
    qi = pl.program_id(2)
    tq, tk, rows = ATT_TQ, ATT_TK, ATT_ROWS
    seq = k_ref.shape[0]
    slope = slopes_ref[h]
    q0 = pl.multiple_of(qi * tq, tq)

    @pl.when(qi == 0)
    def _():
        pos = lax.broadcasted_iota(jnp.int32, (seq, AUG), 0)
        lane = lax.broadcasted_iota(jnp.int32, (seq, AUG), 1)
        feat = jnp.where(lane == 0, pos // CHUNK, jnp.where(lane == 1, pos % CHUNK, jnp.where(lane == 2, 1, 0)))
        kaug_scr[:, :V_HEAD_DIM] = k_ref[...]
        kaug_scr[:, V_HEAD_DIM:] = feat.astype(F32).astype(BF16)
        vaug_scr[:, :V_HEAD_DIM] = v_ref[...]
        vaug_scr[:, V_HEAD_DIM:] = jnp.where(lane == 0, 1.0, 0.0).astype(BF16)

    q = q_ref[...]
    lane = lax.broadcasted_iota(jnp.int32, q.shape, 1)
    zero = jnp.zeros_like(q)
    aug = jnp.where(lane == 0, CHUNK * slope,
                    jnp.where(lane == 1, slope, jnp.where(lane == 2, -slope * q0.astype(F32), 0.0))).astype(BF16)
    qa_scr[:tq, :V_HEAD_DIM] = jnp.where(lane < HEAD_DIM, q, zero)
    qa_scr[tq:, :V_HEAD_DIM] = jnp.where(lane >= HEAD_DIM, q, zero)
    qa_scr[:tq, V_HEAD_DIM:] = aug
    qa_scr[tq:, V_HEAD_DIM:] = aug
    m_scr[...] = jnp.full_like(m_scr, MASK_VALUE)
    acc_scr[...] = jnp.zeros_like(acc_scr)

    def scores(k0):
        return lax.dot_general(qa_scr[...], kaug_scr[pl.ds(k0, tk), :], (((1,), (1,)), ((), ())),
                               preferred_element_type=F32)

    def update(s, k0):
        m_prev = m_scr[...]
        m_new = jnp.maximum(m_prev, jnp.max(s, axis=1, keepdims=True))
        alpha = jnp.exp(m_prev - m_new)
        p = jnp.exp(s - pltpu.repeat(m_new, tk // 128, axis=1))
        m_scr[...] = m_new
        acc_scr[...] = (acc_scr[...] * pltpu.repeat(alpha, 2, axis=1)
                        + _dot(p.astype(BF16), vaug_scr[pl.ds(k0, tk), :]))

    sa_scr[...] = scores(0)

    def body(pair, carry):
        k0 = pl.multiple_of(pair * (2 * tk), 2 * tk)
        sb_scr[...] = scores(k0 + tk)
        update(sa_scr[...], k0)
        sa_scr[...] = scores(k0 + 2 * tk)
        update(sb_scr[...], k0 + tk)
        return carry

    lax.fori_loop(0, qi, body, 0)

    row = lax.broadcasted_iota(jnp.int32, (rows, tk), 0)
    col = lax.broadcasted_iota(jnp.int32, (rows, tk), 1)
    trow = jnp.where(row >= tq, row - tq, row)
    for j in range(tq // tk):
        k0 = pl.multiple_of(q0 + j * tk, tk)
        s = sa_scr[...] if j == 0 else scores(k0)
        ahead = jnp.maximum(col + j * tk - trow, 0).astype(F32)
        s = s - (2.0 * slope) * ahead
        s = jnp.where((col + j * tk) // CHUNK <= trow // CHUNK, s, MASK_VALUE)
        update(s, k0)

    lv = lam_ref[...]
    lam = (jnp.exp(jnp.sum(lv[0:1] * lv[1:2], axis=-1, keepdims=True))
           - jnp.exp(jnp.sum(lv[2:3] * lv[3:4], axis=-1, keepdims=True)) + LAM_INIT)
    acc = acc_scr[...]
    o_n = acc[:, :V_HEAD_DIM] / acc[:, V_HEAD_DIM:V_HEAD_DIM + 1]
    o = o_n[:tq] - lam * o_n[tq:]
    y = o * lax.rsqrt(jnp.mean(o * o, axis=-1, keepdims=True) + LN_EPS)
    o_ref[...] = (y * g_ref[...] * (1.0 - LAM_INIT)).astype(BF16)


def _diff_attn(mix, slopes, lam_vecs, attn_g, batch, seq):
    m = mix.shape[0]
    nq = seq // ATT_TQ
    qc, kc, vc = COL_Q // V_HEAD_DIM, COL_K // V_HEAD_DIM, COL_VAL // V_HEAD_DIM
    return pl.pallas_call(
        _attn_kernel,
        grid=(batch, N_HEADS, nq),
        in_specs=[
            pl.BlockSpec(memory_space=pltpu.SMEM),
            pl.BlockSpec((4, HEAD_DIM), lambda b, h, i: (0, 0)),
            pl.BlockSpec((ATT_TQ, V_HEAD_DIM), lambda b, h, i: (b * nq + i, qc + h)),
            pl.BlockSpec((seq, V_HEAD_DIM), lambda b, h, i: (b, kc + h)),
            pl.BlockSpec((seq, V_HEAD_DIM), lambda b, h, i: (b, vc + h)),
            pl.BlockSpec((1, V_HEAD_DIM), lambda b, h, i: (0, h)),
        ],
        out_specs=pl.BlockSpec((ATT_TQ, V_HEAD_DIM), lambda b, h, i: (b * nq + i, h)),
        out_shape=jax.ShapeDtypeStruct((m, ATTN_WIDTH), BF16),
        scratch_shapes=[
            pltpu.VMEM((seq, V_HEAD_DIM + AUG), BF16),
            pltpu.VMEM((seq, V_HEAD_DIM + AUG), BF16),
            pltpu.VMEM((ATT_ROWS, V_HEAD_DIM + AUG), BF16),
            pltpu.VMEM((ATT_ROWS, ATT_TK), F32),
            pltpu.VMEM((ATT_ROWS, ATT_TK), F32),
            pltpu.VMEM((ATT_ROWS, 128), F32),
            pltpu.VMEM((ATT_ROWS, V_HEAD_DIM + AUG), F32),
        ],
        compiler_params=_params(("parallel", "parallel", "arbitrary")),
        name="diff_attn",
    )(slopes, lam_vecs, mix, mix, mix, attn_g)


MERGE_TM = 512
MERGE_TC = 512


def _merge_kernel(xb_ref, xf_ref, ya_ref, yb_ref, wga_ref, wgb_ref, wa_ref, wb_ref, wo_ref, g_ref, b_ref,
                  o_ref, acc_scr):
    j = pl.program_id(1)

    @pl.when(j == 0)
    def _():
        acc_scr[...] = jnp.zeros_like(acc_scr)

    xb = xb_ref[...]
    merged = (jax.nn.sigmoid(_dot(xb, wga_ref[...])) * _dot(ya_ref[...], wa_ref[...])
              + jax.nn.sigmoid(_dot(xb, wgb_ref[...])) * _dot(yb_ref[...], wb_ref[...]))
    acc_scr[...] += _dot(merged.astype(BF16), wo_ref[...])

    @pl.when(j == pl.num_programs(1) - 1)
    def _():
        o_ref[...] = _layer_norm(ALPHA * xf_ref[...] + acc_scr[...], g_ref[...], b_ref[...])


def _merge_out(xb, xf, ya, yb, w_in, w_a, w_b, w_o, g, b):
    m, d = xf.shape
    nc = d // MERGE_TC
    ga0, gb0 = COL_GA // MERGE_TC, COL_GB // MERGE_TC
    row = lambda i, j: (i, 0)
    return pl.pallas_call(
        _merge_kernel,
        grid=(m // MERGE_TM, nc),
        in_specs=[
            pl.BlockSpec((MERGE_TM, d), row),
            pl.BlockSpec((MERGE_TM, d), row),
            pl.BlockSpec((MERGE_TM, SGU_WIDTH), row),
            pl.BlockSpec((MERGE_TM, ATTN_WIDTH), row),
            pl.BlockSpec((d, MERGE_TC), lambda i, j: (0, ga0 + j)),
            pl.BlockSpec((d, MERGE_TC), lambda i, j: (0, gb0 + j)),
            pl.BlockSpec((SGU_WIDTH, MERGE_TC), lambda i, j: (0, j)),
            pl.BlockSpec((ATTN_WIDTH, MERGE_TC), lambda i, j: (0, j)),
            pl.BlockSpec((MERGE_TC, d), lambda i, j: (j, 0)),
            pl.BlockSpec((1, d), lambda i, j: (0, 0)),
            pl.BlockSpec((1, d), lambda i, j: (0, 0)),
        ],
        out_specs=pl.BlockSpec((MERGE_TM, d), row),
        out_shape=jax.ShapeDtypeStruct((m, d), F32),
        scratch_shapes=[pltpu.VMEM((MERGE_TM, d), F32)],
        compiler_params=_params(("parallel", "arbitrary")),
        name="merge_out",
    )(xb, xf, ya, yb, w_in, w_in, w_a, w_b, w_o, g, b)


PE_TM = 512
PE_TN = 512


def _pe_kernel(xb_ref, xf_ref, p_ref, wg_ref, wp_ref, g_ref, b_ref, o_ref, y_scr):
    j = pl.program_id(1)
    gate = jax.nn.sigmoid(_dot(xb_ref[...], wg_ref[...]))
    emb = _dot(p_ref[...].astype(BF16), wp_ref[...])
    c0 = pl.multiple_of(j * PE_TN, PE_TN)
    y_scr[:, pl.ds(c0, PE_TN)] = ALPHA * xf_ref[:, pl.ds(c0, PE_TN)] + gate * emb

    @pl.when(j == pl.num_programs(1) - 1)
    def _():
        o_ref[...] = _layer_norm(y_scr[...], g_ref[...], b_ref[...])


def _pe_gate(xb, xf, p, w_g, w_p, g, b):
    m, d = xf.shape
    row = lambda i, j: (i, 0)
    return pl.pallas_call(
        _pe_kernel,
        grid=(m // PE_TM, d // PE_TN),
        in_specs=[
            pl.BlockSpec((PE_TM, d), row),
            pl.BlockSpec((PE_TM, d), row),
            pl.BlockSpec((PE_TM, P_DIM), row),
            pl.BlockSpec((d, PE_TN), lambda i, j: (0, j)),
            pl.BlockSpec((P_DIM, PE_TN), lambda i, j: (0, j)),
            pl.BlockSpec((1, d), lambda i, j: (0, 0)),
            pl.BlockSpec((1, d), lambda i, j: (0, 0)),
        ],
        out_specs=pl.BlockSpec((PE_TM, d), row),
        out_shape=jax.ShapeDtypeStruct((m, d), F32),
        scratch_shapes=[pltpu.VMEM((PE_TM, d), F32)],
        compiler_params=_params(("parallel", "arbitrary")),
        name="pe_gate",
    )(xb, xf, p, w_g, w_p, g, b)


def kernel(x, p, ffn1_w_gu, ffn1_w_down, ln1_g, ln1_b, w_in, sgu_ln_g, sgu_ln_b, sgu_w, sgu_b, lam_q1, lam_k1, lam_q2, lam_k2, attn_norm_g, w_branch_a, w_branch_b, w_out, ln2_g, ln2_b, ffn2_w_gu, ffn2_w_down, ln3_g, ln3_b, w_pe_gate, w_pe_proj, ln4_g, ln4_b):
    batch, seq, d = x.shape
    m = batch * seq
    slopes = jnp.asarray(2.0 ** (-8.0 * np.arange(1, N_HEADS + 1) / N_HEADS), dtype=F32)
    xf = x.reshape(m, d)
    for i in range(DEPTH):
        w_in_b = w_in[i].astype(BF16)
        bias_full = jnp.repeat(sgu_b[i].T, SGU_BLOCK, axis=1)
        lam_vecs = jnp.stack([lam_q1[i], lam_k1[i], lam_q2[i], lam_k2[i]]).astype(F32)

        x1f, x1b = _ffn_ln(xf, ffn1_w_gu[i].astype(BF16), ffn1_w_down[i].astype(BF16),
                           ln1_g[i][None], ln1_b[i][None])
        mix = _in_proj(x1b, w_in_b, sgu_ln_g[i][None], sgu_ln_b[i][None])
        y_a = _sgu(mix, sgu_w[i], bias_full)
        y_b = _diff_attn(mix, slopes, lam_vecs, attn_norm_g[i][None], batch, seq)
        x2f = _merge_out(x1b, x1f, y_a, y_b, w_in_b, w_branch_a[i].astype(BF16), w_branch_b[i].astype(BF16),
                            w_out[i].astype(BF16), ln2_g[i][None], ln2_b[i][None])
        x3f, x3b = _ffn_ln(x2f, ffn2_w_gu[i].astype(BF16), ffn2_w_down[i].astype(BF16),
                           ln3_g[i][None], ln3_b[i][None])
        xf = _pe_gate(x3b, x3f, p[i].reshape(m, P_DIM), w_pe_gate[i].astype(BF16), w_pe_proj[i].astype(BF16),
                      ln4_g[i][None], ln4_b[i][None])
    return xf.reshape(batch, seq, d)
```

```python
import functools
import math

import jax
import jax.numpy as jnp
import numpy as np
from jax import lax
from jax.experimental import pallas as pl
from jax.experimental.pallas import tpu as pltpu

D_MODEL = 2048
CHUNK = 64
D_FF = 5632
P_DIM = 256
SGU_BLOCK = 128
SGU_GROUPS = 8
SGU_WIDTH = 1024
N_HEADS = 8
HEAD_DIM = 64
V_HEAD_DIM = 128
QK_WIDTH = 1024
ATTN_WIDTH = 1024
DEPTH = 1
ALPHA = (2 * DEPTH) ** 0.25
LN_EPS = 1e-5
LAM_INIT = 0.8 - 0.6 * math.exp(-0.3 * 0)

COL_U, COL_V, COL_Q, COL_K, COL_VAL = 0, 1024, 2048, 3072, 4096
COL_GA, COL_GB = 5120, 7168
MIX_COLS = 5120

BF16 = jnp.bfloat16
F32 = jnp.float32

VMEM_LIMIT_BYTES = 56 * 1024 * 1024

MASK_VALUE = -1e30


def _params(semantics):
    return pltpu.CompilerParams(dimension_semantics=semantics, vmem_limit_bytes=VMEM_LIMIT_BYTES)


def _dot(a, b):
    return jnp.dot(a, b, preferred_element_type=F32)


def _layer_norm(y, g, b):
    mu = jnp.mean(y, axis=-1, keepdims=True)
    d = y - mu
    var = jnp.mean(d * d, axis=-1, keepdims=True)
    return d * lax.rsqrt(var + LN_EPS) * g + b


FFN_TM = 512
FFN_TF = 512


def _ffn_ln_kernel(x_ref, wg_ref, wu_ref, wd_ref, g_ref, b_ref, of_ref, ob_ref, xb_scr, acc_scr):
    j = pl.program_id(1)

    @pl.when(j == 0)
    def _():
        xb_scr[...] = x_ref[...].astype(BF16)
        acc_scr[...] = jnp.zeros_like(acc_scr)

    xb = xb_scr[...]
    gate = _dot(xb, wg_ref[...])
    up = _dot(xb, wu_ref[...])
    h = (gate * jax.nn.sigmoid(gate)) * up
    acc_scr[...] += _dot(h.astype(BF16), wd_ref[...])

    @pl.when(j == pl.num_programs(1) - 1)
    def _():
        y = ALPHA * x_ref[...] + 0.5 * acc_scr[...]
        out = _layer_norm(y, g_ref[...], b_ref[...])
        of_ref[...] = out
        ob_ref[...] = out.astype(BF16)


def _ffn_ln(x, w_gu, w_down, g, b):
    m, d = x.shape
    f = w_down.shape[0]
    nf = f // FFN_TF
    return pl.pallas_call(
        _ffn_ln_kernel,
        grid=(m // FFN_TM, nf),
        in_specs=[
            pl.BlockSpec((FFN_TM, d), lambda i, j: (i, 0)),
            pl.BlockSpec((d, FFN_TF), lambda i, j: (0, j)),
            pl.BlockSpec((d, FFN_TF), lambda i, j: (0, nf + j)),
            pl.BlockSpec((FFN_TF, d), lambda i, j: (j, 0)),
            pl.BlockSpec((1, d), lambda i, j: (0, 0)),
            pl.BlockSpec((1, d), lambda i, j: (0, 0)),
        ],
        out_specs=[
            pl.BlockSpec((FFN_TM, d), lambda i, j: (i, 0)),
            pl.BlockSpec((FFN_TM, d), lambda i, j: (i, 0)),
        ],
        out_shape=[jax.ShapeDtypeStruct((m, d), F32), jax.ShapeDtypeStruct((m, d), BF16)],
        scratch_shapes=[pltpu.VMEM((FFN_TM, d), BF16), pltpu.VMEM((FFN_TM, d), F32)],
        compiler_params=_params(("parallel", "arbitrary")),
        name="ffn_ln",
    )(x, w_gu, w_gu, w_down, g, b)


PROJ_TM = 512
PROJ_TN = 1024


def _in_proj_kernel(x_ref, w_ref, g_ref, b_ref, o_ref):
    j = pl.program_id(1)
    acc = _dot(x_ref[...], w_ref[...])

    @pl.when(j == COL_U // PROJ_TN)
    def _():
        o_ref[...] = jax.nn.gelu(acc).astype(BF16)

    @pl.when(j == COL_V // PROJ_TN)
    def _():
        o_ref[...] = _layer_norm(jax.nn.gelu(acc), g_ref[...], b_ref[...]).astype(BF16)

    @pl.when(j == COL_Q // PROJ_TN)
    def _():
        o_ref[...] = (acc * HEAD_DIM ** -0.5).astype(BF16)

    @pl.when(j >= COL_K // PROJ_TN)
    def _():
        o_ref[...] = acc.astype(BF16)


def _in_proj(xb, w_in, sgu_g, sgu_b):
    m, d = xb.shape
    return pl.pallas_call(
        _in_proj_kernel,
        grid=(m // PROJ_TM, MIX_COLS // PROJ_TN),
        in_specs=[
            pl.BlockSpec((PROJ_TM, d), lambda i, j: (i, 0)),
            pl.BlockSpec((d, PROJ_TN), lambda i, j: (0, j)),
            pl.BlockSpec((1, SGU_WIDTH), lambda i, j: (0, 0)),
            pl.BlockSpec((1, SGU_WIDTH), lambda i, j: (0, 0)),
        ],
        out_specs=pl.BlockSpec((PROJ_TM, PROJ_TN), lambda i, j: (i, j)),
        out_shape=jax.ShapeDtypeStruct((m, MIX_COLS), BF16),
        compiler_params=_params(("parallel", "arbitrary")),
        name="in_proj",
    )(xb, w_in, sgu_g, sgu_b)


SGU_TM = 512


def _sgu_kernel(u_ref, v_ref, w_ref, bias_ref, o_ref):
    t = lax.broadcasted_iota(jnp.int32, (SGU_BLOCK, SGU_BLOCK), 0)
    s = lax.broadcasted_iota(jnp.int32, (SGU_BLOCK, SGU_BLOCK), 1)
    allowed = (s // CHUNK) <= (t // CHUNK)
    for g in range(SGU_GROUPS):
        wm = jnp.where(allowed, w_ref[g], 0.0).astype(BF16)
        cols = slice(g * SGU_BLOCK, (g + 1) * SGU_BLOCK)
        bias = bias_ref[:, cols]
        for blk in range(SGU_TM // SGU_BLOCK):
            rows = slice(blk * SGU_BLOCK, (blk + 1) * SGU_BLOCK)
            mix = _dot(wm, v_ref[rows, cols]) + bias
            o_ref[rows, cols] = (u_ref[rows, cols].astype(F32) * mix).astype(BF16)


def _sgu(mix, sgu_w, bias_full):
    m = mix.shape[0]
    return pl.pallas_call(
        _sgu_kernel,
        grid=(m // SGU_TM,),
        in_specs=[
            pl.BlockSpec((SGU_TM, SGU_WIDTH), lambda i: (i, COL_U // SGU_WIDTH)),
            pl.BlockSpec((SGU_TM, SGU_WIDTH), lambda i: (i, COL_V // SGU_WIDTH)),
            pl.BlockSpec((SGU_GROUPS, SGU_BLOCK, SGU_BLOCK), lambda i: (0, 0, 0)),
            pl.BlockSpec((SGU_BLOCK, SGU_WIDTH), lambda i: (0, 0)),
        ],
        out_specs=pl.BlockSpec((SGU_TM, SGU_WIDTH), lambda i: (i, 0)),
        out_shape=jax.ShapeDtypeStruct((m, SGU_WIDTH), BF16),
        compiler_params=_params(("parallel",)),
        name="sgu",
    )(mix, mix, sgu_w, bias_full)


ATT_TK = 512
ATT_TQ = 2 * ATT_TK
ATT_ROWS = 2 * ATT_TQ
AUG = 128


def _lane_tile(x, n):
    return jnp.concatenate([x] * n, axis=1)


def _attn_kernel(slopes_ref, lam_ref, q_ref, k_ref, v_ref, g_ref, o_ref,
                 kaug_scr, vaug_scr, diag_scr, qa_scr, sa_scr, sb_scr, m_scr, acc_scr):
    h = pl.program_id(1)
    qi = pl.program_id(2)
    tq, tk, rows = ATT_TQ, ATT_TK, ATT_ROWS
    seq = k_ref.shape[0]
    slope = slopes_ref[h]
    q0 = pl.multiple_of(qi * tq, tq)

    @pl.when(qi == 0)
    def _():
        pos = lax.broadcasted_iota(jnp.int32, (seq, AUG), 0)
        lane = lax.broadcasted_iota(jnp.int32, (seq, AUG), 1)
        feat = jnp.where(lane == 0, pos // CHUNK, jnp.where(lane == 1, pos % CHUNK, jnp.where(lane == 2, 1, 0)))
        kaug_scr[:, :V_HEAD_DIM] = k_ref[...]
        kaug_scr[:, V_HEAD_DIM:] = feat.astype(F32).astype(BF16)
        vaug_scr[:, :V_HEAD_DIM] = v_ref[...]
        vaug_scr[:, V_HEAD_DIM:] = jnp.ones((seq, AUG), BF16)
        t = lax.broadcasted_iota(jnp.int32, (tk, tk), 0)
        c = lax.broadcasted_iota(jnp.int32, (tk, tk), 1)
        ahead = jnp.maximum(c - t, 0).astype(F32)
        diag_scr[...] = jnp.where(c // CHUNK <= t // CHUNK, (-2.0 * slope) * ahead, MASK_VALUE)

    lane = lax.broadcasted_iota(jnp.int32, (tk, V_HEAD_DIM), 1)
    aug = jnp.where(lane == 0, CHUNK * slope,
                    jnp.where(lane == 1, slope, jnp.where(lane == 2, -slope * q0.astype(F32), 0.0))).astype(BF16)
    for half in range(2):
        q = q_ref[half * tk:(half + 1) * tk, :]
        zero = jnp.zeros_like(q)
        r1, r2 = 2 * half * tk, (2 * half + 1) * tk
        qa_scr[r1:r1 + tk, :V_HEAD_DIM] = jnp.where(lane < HEAD_DIM, q, zero)
        qa_scr[r2:r2 + tk, :V_HEAD_DIM] = jnp.where(lane >= HEAD_DIM, q, zero)
        qa_scr[r1:r1 + tk, V_HEAD_DIM:] = aug
        qa_scr[r2:r2 + tk, V_HEAD_DIM:] = aug
    m_scr[...] = jnp.full_like(m_scr, MASK_VALUE)
    acc_scr[...] = jnp.zeros_like(acc_scr)

    def scores(k0, r0=0):
        return lax.dot_general(qa_scr[r0:, :], kaug_scr[pl.ds(k0, tk), :], (((1,), (1,)), ((), ())),
                               preferred_element_type=F32)

    def update(s, k0, rs=slice(None)):
        width = s.shape[1]
        m_prev = m_scr[rs, :]
        m_new = jnp.maximum(m_prev, jnp.max(s, axis=1, keepdims=True))
        alpha = jnp.exp(m_prev - m_new)
        p = jnp.exp(s - _lane_tile(m_new, width // 128))
        m_scr[rs, :] = m_new
        acc_scr[rs, :] = (acc_scr[rs, :] * _lane_tile(alpha, 2)
                          + _dot(p.astype(BF16), vaug_scr[pl.ds(k0, width), :]))

    sa_scr[...] = scores(0)

    def body(pair, carry):
        k0 = pl.multiple_of(pair * (2 * tk), 2 * tk)
        sb_scr[...] = scores(k0 + tk)
        update(sa_scr[...], k0)
        sa_scr[...] = scores(k0 + 2 * tk)
        update(sb_scr[...], k0 + tk)
        return carry

    lax.fori_loop(0, qi, body, 0)

    first, second = slice(0, 2 * tk), slice(2 * tk, rows)
    diag2 = jnp.concatenate([diag_scr[...], diag_scr[...]], axis=0)
    sb_scr[second, :] = scores(q0 + tk, 2 * tk)
    update(sa_scr[first, :] + diag2, q0, first)
    update(jnp.concatenate([sa_scr[second, :], sb_scr[second, :] + diag2], axis=1), q0, second)

    lv = lam_ref[...]
    lam = (jnp.exp(jnp.sum(lv[0:1] * lv[1:2], axis=-1, keepdims=True))
           - jnp.exp(jnp.sum(lv[2:3] * lv[3:4], axis=-1, keepdims=True)) + LAM_INIT)
    for half in range(2):
        r1, r2 = 2 * half * tk, (2 * half + 1) * tk
        a1, a2 = acc_scr[r1:r1 + tk, :], acc_scr[r2:r2 + tk, :]
        o = a1[:, :V_HEAD_DIM] / a1[:, V_HEAD_DIM:] - lam * (a2[:, :V_HEAD_DIM] / a2[:, V_HEAD_DIM:])
        y = o * lax.rsqrt(jnp.mean(o * o, axis=-1, keepdims=True) + LN_EPS)
        o_ref[half * tk:(half + 1) * tk, :] = (y * g_ref[...] * (1.0 - LAM_INIT)).astype(BF16)


def _diff_attn(mix, slopes, lam_vecs, attn_g, batch, seq):
    m = mix.shape[0]
    nq = seq // ATT_TQ
    qc, kc, vc = COL_Q // V_HEAD_DIM, COL_K // V_HEAD_DIM, COL_VAL // V_HEAD_DIM
    return pl.pallas_call(
        _attn_kernel,
        grid=(batch, N_HEADS, nq),
        in_specs=[
            pl.BlockSpec(memory_space=pltpu.SMEM),
            pl.BlockSpec((4, HEAD_DIM), lambda b, h, i: (0, 0)),
            pl.BlockSpec((ATT_TQ, V_HEAD_DIM), lambda b, h, i: (b * nq + i, qc + h)),
            pl.BlockSpec((seq, V_HEAD_DIM), lambda b, h, i: (b, kc + h)),
            pl.BlockSpec((seq, V_HEAD_DIM), lambda b, h, i: (b, vc + h)),
            pl.BlockSpec((1, V_HEAD_DIM), lambda b, h, i: (0, h)),
        ],
        out_specs=pl.BlockSpec((ATT_TQ, V_HEAD_DIM), lambda b, h, i: (b * nq + i, h)),
        out_shape=jax.ShapeDtypeStruct((m, ATTN_WIDTH), BF16),
        scratch_shapes=[
            pltpu.VMEM((seq, V_HEAD_DIM + AUG), BF16),
            pltpu.VMEM((seq, V_HEAD_DIM + AUG), BF16),
            pltpu.VMEM((ATT_TK, ATT_TK), F32),
            pltpu.VMEM((ATT_ROWS, V_HEAD_DIM + AUG), BF16),
            pltpu.VMEM((ATT_ROWS, ATT_TK), F32),
            pltpu.VMEM((ATT_ROWS, ATT_TK), F32),
            pltpu.VMEM((ATT_ROWS, 128), F32),
            pltpu.VMEM((ATT_ROWS, V_HEAD_DIM + AUG), F32),
        ],
        compiler_params=_params(("parallel", "parallel", "arbitrary")),
        name="diff_attn",
    )(slopes, lam_vecs, mix, mix, mix, attn_g)


MERGE_TM = 512
MERGE_TC = 512


def _merge_kernel(xb_ref, xf_ref, ya_ref, yb_ref, wga_ref, wgb_ref, wa_ref, wb_ref, wo_ref, g_ref, b_ref,
                  o_ref, acc_scr):
    j = pl.program_id(1)

    @pl.when(j == 0)
    def _():
        acc_scr[...] = jnp.zeros_like(acc_scr)

    xb = xb_ref[...]
    merged = (jax.nn.sigmoid(_dot(xb, wga_ref[...])) * _dot(ya_ref[...], wa_ref[...])
              + jax.nn.sigmoid(_dot(xb, wgb_ref[...])) * _dot(yb_ref[...], wb_ref[...]))
    acc_scr[...] += _dot(merged.astype(BF16), wo_ref[...])

    @pl.when(j == pl.num_programs(1) - 1)
    def _():
        o_ref[...] = _layer_norm(ALPHA * xf_ref[...] + acc_scr[...], g_ref[...], b_ref[...])


def _merge_out(xb, xf, ya, yb, w_in, w_a, w_b, w_o, g, b):
    m, d = xf.shape
    nc = d // MERGE_TC
    ga0, gb0 = COL_GA // MERGE_TC, COL_GB // MERGE_TC
    row = lambda i, j: (i, 0)
    return pl.pallas_call(
        _merge_kernel,
        grid=(m // MERGE_TM, nc),
        in_specs=[
            pl.BlockSpec((MERGE_TM, d), row),
            pl.BlockSpec((MERGE_TM, d), row),
            pl.BlockSpec((MERGE_TM, SGU_WIDTH), row),
            pl.BlockSpec((MERGE_TM, ATTN_WIDTH), row),
            pl.BlockSpec((d, MERGE_TC), lambda i, j: (0, ga0 + j)),
            pl.BlockSpec((d, MERGE_TC), lambda i, j: (0, gb0 + j)),
            pl.BlockSpec((SGU_WIDTH, MERGE_TC), lambda i, j: (0, j)),
            pl.BlockSpec((ATTN_WIDTH, MERGE_TC), lambda i, j: (0, j)),
            pl.BlockSpec((MERGE_TC, d), lambda i, j: (j, 0)),
            pl.BlockSpec((1, d), lambda i, j: (0, 0)),
            pl.BlockSpec((1, d), lambda i, j: (0, 0)),
        ],
        out_specs=pl.BlockSpec((MERGE_TM, d), row),
        out_shape=jax.ShapeDtypeStruct((m, d), F32),
        scratch_shapes=[pltpu.VMEM((MERGE_TM, d), F32)],
        compiler_params=_params(("parallel", "arbitrary")),
        name="merge_out",
    )(xb, xf, ya, yb, w_in, w_in, w_a, w_b, w_o, g, b)


PE_TM = 512
PE_TN = 512


def _pe_kernel(xb_ref, xf_ref, p_ref, wg_ref, wp_ref, g_ref, b_ref, o_ref, y_scr):
    j = pl.program_id(1)
    gate = jax.nn.sigmoid(_dot(xb_ref[...], wg_ref[...]))
    emb = _dot(p_ref[...].astype(BF16), wp_ref[...])
    c0 = pl.multiple_of(j * PE_TN, PE_TN)
    y_scr[:, pl.ds(c0, PE_TN)] = ALPHA * xf_ref[:, pl.ds(c0, PE_TN)] + gate * emb

    @pl.when(j == pl.num_programs(1) - 1)
    def _():
        o_ref[...] = _layer_norm(y_scr[...], g_ref[...], b_ref[...])


def _pe_gate(xb, xf, p, w_g, w_p, g, b):
    m, d = xf.shape
    row = lambda i, j: (i, 0)
    return pl.pallas_call(
        _pe_kernel,
        grid=(m // PE_TM, d // PE_TN),
        in_specs=[
            pl.BlockSpec((PE_TM, d), row),
            pl.BlockSpec((PE_TM, d), row),
            pl.BlockSpec((PE_TM, P_DIM), row),
            pl.BlockSpec((d, PE_TN), lambda i, j: (0, j)),
            pl.BlockSpec((P_DIM, PE_TN), lambda i, j: (0, j)),
            pl.BlockSpec((1, d), lambda i, j: (0, 0)),
            pl.BlockSpec((1, d), lambda i, j: (0, 0)),
        ],
        out_specs=pl.BlockSpec((PE_TM, d), row),
        out_shape=jax.ShapeDtypeStruct((m, d), F32),
        scratch_shapes=[pltpu.VMEM((PE_TM, d), F32)],
        compiler_params=_params(("parallel", "arbitrary")),
        name="pe_gate",
    )(xb, xf, p, w_g, w_p, g, b)


def kernel(x, p, ffn1_w_gu, ffn1_w_down, ln1_g, ln1_b, w_in, sgu_ln_g, sgu_ln_b, sgu_w, sgu_b, lam_q1, lam_k1, lam_q2, lam_k2, attn_norm_g, w_branch_a, w_branch_b, w_out, ln2_g, ln2_b, ffn2_w_gu, ffn2_w_down, ln3_g, ln3_b, w_pe_gate, w_pe_proj, ln4_g, ln4_b):
    batch, seq, d = x.shape
    m = batch * seq
    slopes = jnp.asarray(2.0 ** (-8.0 * np.arange(1, N_HEADS + 1) / N_HEADS), dtype=F32)
    xf = x.reshape(m, d)
    for i in range(DEPTH):
        w_in_b = w_in[i].astype(BF16)
        bias_full = jnp.repeat(sgu_b[i].T, SGU_BLOCK, axis=1)
        lam_vecs = jnp.stack([lam_q1[i], lam_k1[i], lam_q2[i], lam_k2[i]]).astype(F32)

        x1f, x1b = _ffn_ln(xf, ffn1_w_gu[i].astype(BF16), ffn1_w_down[i].astype(BF16),
                           ln1_g[i][None], ln1_b[i][None])
        mix = _in_proj(x1b, w_in_b, sgu_ln_g[i][None], sgu_ln_b[i][None])
        y_a = _sgu(mix, sgu_w[i], bias_full)
        y_b = _diff_attn(mix, slopes, lam_vecs, attn_norm_g[i][None], batch, seq)
        x2f = _merge_out(x1b, x1f, y_a, y_b, w_in_b, w_branch_a[i].astype(BF16), w_branch_b[i].astype(BF16),
                            w_out[i].astype(BF16), ln2_g[i][None], ln2_b[i][None])
        x3f, x3b = _ffn_ln(x2f, ffn2_w_gu[i].astype(BF16), ffn2_w_down[i].astype(BF16),
                           ln3_g[i][None], ln3_b[i][None])
        xf = _pe_gate(x3b, x3f, p[i].reshape(m, P_DIM), w_pe_gate[i].astype(BF16), w_pe_proj[i].astype(BF16),
                      ln4_g[i][None], ln4_b[i][None])
    return xf.reshape(batch, seq, d)
```

```python
import functools
import math

import jax
import jax.numpy as jnp
import numpy as np
from jax import lax
from jax.experimental import pallas as pl
from jax.experimental.pallas import tpu as pltpu

D_MODEL = 2048
CHUNK = 64
D_FF = 5632
P_DIM = 256
SGU_BLOCK = 128
SGU_GROUPS = 8
SGU_WIDTH = 1024
N_HEADS = 8
HEAD_DIM = 64
V_HEAD_DIM = 128
QK_WIDTH = 1024
ATTN_WIDTH = 1024
DEPTH = 1
ALPHA = (2 * DEPTH) ** 0.25
LN_EPS = 1e-5
LAM_INIT = 0.8 - 0.6 * math.exp(-0.3 * 0)

COL_U, COL_V, COL_Q, COL_K, COL_VAL = 0, 1024, 2048, 3072, 4096
COL_GA, COL_GB = 5120, 7168
MIX_COLS = 5120

BF16 = jnp.bfloat16
F32 = jnp.float32

VMEM_LIMIT_BYTES = 56 * 1024 * 1024

MASK_VALUE = -1e30


def _params(semantics):
    return pltpu.CompilerParams(dimension_semantics=semantics, vmem_limit_bytes=VMEM_LIMIT_BYTES)


def _dot(a, b):
    return jnp.dot(a, b, preferred_element_type=F32)


def _layer_norm(y, g, b):
    mu = jnp.mean(y, axis=-1, keepdims=True)
    d = y - mu
    var = jnp.mean(d * d, axis=-1, keepdims=True)
    return d * lax.rsqrt(var + LN_EPS) * g + b


FFN_TM = 512
FFN_TF = 512


def _deferred_out_index(i, j):
    return (jnp.where(j == 0, jnp.maximum(i - 1, 0), i), 0)


def _ffn_ln_kernel(x_ref, wg_ref, wu_ref, wd_ref, g_ref, b_ref, of_ref, ob_ref, xb_scr, acc_scr, pend_scr):
    i, j = pl.program_id(0), pl.program_id(1)
    last = pl.num_programs(1) - 1

    def partial():
        xb = xb_scr[...]
        gate = _dot(xb, wg_ref[...])
        up = _dot(xb, wu_ref[...])
        h = (gate * jax.nn.sigmoid(gate)) * up
        return _dot(h.astype(BF16), wd_ref[...])

    def emit_norm():
        out = _layer_norm(pend_scr[...], g_ref[...], b_ref[...])
        of_ref[...] = out
        ob_ref[...] = out.astype(BF16)

    @pl.when((i == 0) & (j == 0))
    def _():
        pend_scr[...] = jnp.zeros_like(pend_scr)

    @pl.when(j == 0)
    def _():
        xb_scr[...] = x_ref[...].astype(BF16)
        acc_scr[...] = partial()
        emit_norm()

    @pl.when((j > 0) & (j < last))
    def _():
        acc_scr[...] += partial()

    @pl.when(j == last)
    def _():
        pend_scr[...] = ALPHA * x_ref[...] + 0.5 * (acc_scr[...] + partial())

    @pl.when((j == last) & (i == pl.num_programs(0) - 1))
    def _():
        emit_norm()


def _ffn_ln(x, w_gu, w_down, g, b):
    m, d = x.shape
    f = w_down.shape[0]
    nf = f // FFN_TF
    return pl.pallas_call(
        _ffn_ln_kernel,
        grid=(m // FFN_TM, nf),
        in_specs=[
            pl.BlockSpec((FFN_TM, d), lambda i, j: (i, 0)),
            pl.BlockSpec((d, FFN_TF), lambda i, j: (0, j)),
            pl.BlockSpec((d, FFN_TF), lambda i, j: (0, nf + j)),
            pl.BlockSpec((FFN_TF, d), lambda i, j: (j, 0)),
            pl.BlockSpec((1, d), lambda i, j: (0, 0)),
            pl.BlockSpec((1, d), lambda i, j: (0, 0)),
        ],
        out_specs=[
            pl.BlockSpec((FFN_TM, d), _deferred_out_index),
            pl.BlockSpec((FFN_TM, d), _deferred_out_index),
        ],
        out_shape=[jax.ShapeDtypeStruct((m, d), F32), jax.ShapeDtypeStruct((m, d), BF16)],
        scratch_shapes=[pltpu.VMEM((FFN_TM, d), BF16), pltpu.VMEM((FFN_TM, d), F32),
                        pltpu.VMEM((FFN_TM, d), F32)],
        compiler_params=_params(("arbitrary", "arbitrary")),
        name="ffn_ln",
    )(x, w_gu, w_gu, w_down, g, b)


PROJ_TM = 512
PROJ_TN = 1024


def _in_proj_kernel(x_ref, w_ref, g_ref, b_ref, o_ref):
    j = pl.program_id(1)

    def proj():
        return _dot(x_ref[...], w_ref[...])

    @pl.when(j == COL_U // PROJ_TN)
    def _():
        o_ref[...] = jax.nn.gelu(proj()).astype(BF16)

    @pl.when(j == COL_V // PROJ_TN)
    def _():
        o_ref[...] = _layer_norm(jax.nn.gelu(proj()), g_ref[...], b_ref[...]).astype(BF16)

    @pl.when(j == COL_Q // PROJ_TN)
    def _():
        o_ref[...] = (proj() * HEAD_DIM ** -0.5).astype(BF16)

    @pl.when(j >= COL_K // PROJ_TN)
    def _():
        o_ref[...] = proj().astype(BF16)


def _in_proj(xb, w_in, sgu_g, sgu_b):
    m, d = xb.shape
    return pl.pallas_call(
        _in_proj_kernel,
        grid=(m // PROJ_TM, MIX_COLS // PROJ_TN),
        in_specs=[
            pl.BlockSpec((PROJ_TM, d), lambda i, j: (i, 0)),
            pl.BlockSpec((d, PROJ_TN), lambda i, j: (0, j)),
            pl.BlockSpec((1, SGU_WIDTH), lambda i, j: (0, 0)),
            pl.BlockSpec((1, SGU_WIDTH), lambda i, j: (0, 0)),
        ],
        out_specs=pl.BlockSpec((PROJ_TM, PROJ_TN), lambda i, j: (i, j)),
        out_shape=jax.ShapeDtypeStruct((m, MIX_COLS), BF16),
        compiler_params=_params(("parallel", "arbitrary")),
        name="in_proj",
    )(xb, w_in, sgu_g, sgu_b)


SGU_TM = 512


def _sgu_kernel(u_ref, v_ref, w_ref, bias_ref, o_ref):
    t = lax.broadcasted_iota(jnp.int32, (SGU_BLOCK, SGU_BLOCK), 0)
    s = lax.broadcasted_iota(jnp.int32, (SGU_BLOCK, SGU_BLOCK), 1)
    allowed = (s // CHUNK) <= (t // CHUNK)
    for g in range(SGU_GROUPS):
        wm = jnp.where(allowed, w_ref[g], 0.0).astype(BF16)
        cols = slice(g * SGU_BLOCK, (g + 1) * SGU_BLOCK)
        bias = bias_ref[:, cols]
        for blk in range(SGU_TM // SGU_BLOCK):
            rows = slice(blk * SGU_BLOCK, (blk + 1) * SGU_BLOCK)
            mix = _dot(wm, v_ref[rows, cols]) + bias
            o_ref[rows, cols] = (u_ref[rows, cols].astype(F32) * mix).astype(BF16)


def _sgu(mix, sgu_w, bias_full):
    m = mix.shape[0]
    return pl.pallas_call(
        _sgu_kernel,
        grid=(m // SGU_TM,),
        in_specs=[
            pl.BlockSpec((SGU_TM, SGU_WIDTH), lambda i: (i, COL_U // SGU_WIDTH)),
            pl.BlockSpec((SGU_TM, SGU_WIDTH), lambda i: (i, COL_V // SGU_WIDTH)),
            pl.BlockSpec((SGU_GROUPS, SGU_BLOCK, SGU_BLOCK), lambda i: (0, 0, 0)),
            pl.BlockSpec((SGU_BLOCK, SGU_WIDTH), lambda i: (0, 0)),
        ],
        out_specs=pl.BlockSpec((SGU_TM, SGU_WIDTH), lambda i: (i, 0)),
        out_shape=jax.ShapeDtypeStruct((m, SGU_WIDTH), BF16),
        compiler_params=_params(("parallel",)),
        name="sgu",
    )(mix, mix, sgu_w, bias_full)


ATT_TK = 512
ATT_TQ = 2 * ATT_TK
ATT_ROWS = 2 * ATT_TQ
AUG = 128


def _lane_tile(x, n):
    return jnp.concatenate([x] * n, axis=1)


def _attn_kernel(slopes_ref, lam_ref, q_ref, k_ref, v_ref, g_ref, o_ref,
                 kaug_scr, vaug_scr, diag_scr, qa_scr, sa_scr, sb_scr, m_scr, acc_scr):
    h = pl.program_id(1)
    qi = pl.program_id(2)
    tq, tk, rows = ATT_TQ, ATT_TK, ATT_ROWS
    seq = k_ref.shape[0]
    slope = slopes_ref[h]
    q0 = pl.multiple_of(qi * tq, tq)

    @pl.when(qi == 0)
    def _():
        pos = lax.broadcasted_iota(jnp.int32, (seq, AUG), 0)
        lane = lax.broadcasted_iota(jnp.int32, (seq, AUG), 1)
        feat = jnp.where(lane == 0, pos // CHUNK, jnp.where(lane == 1, pos % CHUNK, jnp.where(lane == 2, 1, 0)))
        kaug_scr[:, :V_HEAD_DIM] = k_ref[...]
        kaug_scr[:, V_HEAD_DIM:] = feat.astype(F32).astype(BF16)
        vaug_scr[:, :V_HEAD_DIM] = v_ref[...]
        vaug_scr[:, V_HEAD_DIM:] = jnp.ones((seq, AUG), BF16)
        t = lax.broadcasted_iota(jnp.int32, (tk, tk), 0)
        c = lax.broadcasted_iota(jnp.int32, (tk, tk), 1)
        ahead = jnp.maximum(c - t, 0).astype(F32)
        diag_scr[...] = jnp.where(c // CHUNK <= t // CHUNK, (-2.0 * slope) * ahead, MASK_VALUE)

    lane = lax.broadcasted_iota(jnp.int32, (tk, V_HEAD_DIM), 1)
    aug = jnp.where(lane == 0, CHUNK * slope,
                    jnp.where(lane == 1, slope, jnp.where(lane == 2, -slope * q0.astype(F32), 0.0))).astype(BF16)
    for half in range(2):
        q = q_ref[half * tk:(half + 1) * tk, :]
        zero = jnp.zeros_like(q)
        r1, r2 = 2 * half * tk, (2 * half + 1) * tk
        qa_scr[r1:r1 + tk, :V_HEAD_DIM] = jnp.where(lane < HEAD_DIM, q, zero)
        qa_scr[r2:r2 + tk, :V_HEAD_DIM] = jnp.where(lane >= HEAD_DIM, q, zero)
        qa_scr[r1:r1 + tk, V_HEAD_DIM:] = aug
        qa_scr[r2:r2 + tk, V_HEAD_DIM:] = aug
    m_scr[...] = jnp.full_like(m_scr, MASK_VALUE)
    acc_scr[...] = jnp.zeros_like(acc_scr)

    def scores(k0, r0=0):
        return lax.dot_general(qa_scr[r0:, :], kaug_scr[pl.ds(k0, tk), :], (((1,), (1,)), ((), ())),
                               preferred_element_type=F32)

    def update(s, k0, rs=slice(None)):
        width = s.shape[1]
        m_prev = m_scr[rs, :]
        m_new = jnp.maximum(m_prev, jnp.max(s, axis=1, keepdims=True))
        alpha = jnp.exp(m_prev - m_new)
        p = jnp.exp(s - _lane_tile(m_new, width // 128))
        m_scr[rs, :] = m_new
        acc_scr[rs, :] = (acc_scr[rs, :] * _lane_tile(alpha, 2)
                          + _dot(p.astype(BF16), vaug_scr[pl.ds(k0, width), :]))

    sa_scr[...] = scores(0)

    def body(pair, carry):
        k0 = pl.multiple_of(pair * (2 * tk), 2 * tk)
        sb_scr[...] = scores(k0 + tk)
        update(sa_scr[...], k0)
        sa_scr[...] = scores(k0 + 2 * tk)
        update(sb_scr[...], k0 + tk)
        return carry

    lax.fori_loop(0, qi, body, 0)

    first, second = slice(0, 2 * tk), slice(2 * tk, rows)
    diag2 = jnp.concatenate([diag_scr[...], diag_scr[...]], axis=0)
    sb_scr[second, :] = scores(q0 + tk, 2 * tk)
    update(sa_scr[first, :] + diag2, q0, first)
    update(jnp.concatenate([sa_scr[second, :], sb_scr[second, :] + diag2], axis=1), q0, second)

    lv = lam_ref[...]
    lam = (jnp.exp(jnp.sum(lv[0:1] * lv[1:2], axis=-1, keepdims=True))
           - jnp.exp(jnp.sum(lv[2:3] * lv[3:4], axis=-1, keepdims=True)) + LAM_INIT)
    for half in range(2):
        r1, r2 = 2 * half * tk, (2 * half + 1) * tk
        a1, a2 = acc_scr[r1:r1 + tk, :], acc_scr[r2:r2 + tk, :]
        o = a1[:, :V_HEAD_DIM] / a1[:, V_HEAD_DIM:] - lam * (a2[:, :V_HEAD_DIM] / a2[:, V_HEAD_DIM:])
        y = o * lax.rsqrt(jnp.mean(o * o, axis=-1, keepdims=True) + LN_EPS)
        o_ref[half * tk:(half + 1) * tk, :] = (y * g_ref[...] * (1.0 - LAM_INIT)).astype(BF16)


def _diff_attn(mix, slopes, lam_vecs, attn_g, batch, seq):
    m = mix.shape[0]
    nq = seq // ATT_TQ
    qc, kc, vc = COL_Q // V_HEAD_DIM, COL_K // V_HEAD_DIM, COL_VAL // V_HEAD_DIM
    return pl.pallas_call(
        _attn_kernel,
        grid=(batch, N_HEADS, nq),
        in_specs=[
            pl.BlockSpec(memory_space=pltpu.SMEM),
            pl.BlockSpec((4, HEAD_DIM), lambda b, h, i: (0, 0)),
            pl.BlockSpec((ATT_TQ, V_HEAD_DIM), lambda b, h, i: (b * nq + i, qc + h)),
            pl.BlockSpec((seq, V_HEAD_DIM), lambda b, h, i: (b, kc + h)),
            pl.BlockSpec((seq, V_HEAD_DIM), lambda b, h, i: (b, vc + h)),
            pl.BlockSpec((1, V_HEAD_DIM), lambda b, h, i: (0, h)),
        ],
        out_specs=pl.BlockSpec((ATT_TQ, V_HEAD_DIM), lambda b, h, i: (b * nq + i, h)),
        out_shape=jax.ShapeDtypeStruct((m, ATTN_WIDTH), BF16),
        scratch_shapes=[
            pltpu.VMEM((seq, V_HEAD_DIM + AUG), BF16),
            pltpu.VMEM((seq, V_HEAD_DIM + AUG), BF16),
            pltpu.VMEM((ATT_TK, ATT_TK), F32),
            pltpu.VMEM((ATT_ROWS, V_HEAD_DIM + AUG), BF16),
            pltpu.VMEM((ATT_ROWS, ATT_TK), F32),
            pltpu.VMEM((ATT_ROWS, ATT_TK), F32),
            pltpu.VMEM((ATT_ROWS, 128), F32),
            pltpu.VMEM((ATT_ROWS, V_HEAD_DIM + AUG), F32),
        ],
        compiler_params=_params(("parallel", "parallel", "arbitrary")),
        name="diff_attn",
    )(slopes, lam_vecs, mix, mix, mix, attn_g)


MERGE_TM = 512
MERGE_TC = 512


def _merge_kernel(xb_ref, xf_ref, ya_ref, yb_ref, wga_ref, wgb_ref, wa_ref, wb_ref, wo_ref, g_ref, b_ref,
                  o_ref, acc_scr, pend_scr):
    i, j = pl.program_id(0), pl.program_id(1)
    last = pl.num_programs(1) - 1

    def partial():
        xb = xb_ref[...]
        merged = (jax.nn.sigmoid(_dot(xb, wga_ref[...])) * _dot(ya_ref[...], wa_ref[...])
                  + jax.nn.sigmoid(_dot(xb, wgb_ref[...])) * _dot(yb_ref[...], wb_ref[...]))
        return _dot(merged.astype(BF16), wo_ref[...])

    def emit_norm():
        o_ref[...] = _layer_norm(pend_scr[...], g_ref[...], b_ref[...])

    @pl.when((i == 0) & (j == 0))
    def _():
        pend_scr[...] = jnp.zeros_like(pend_scr)

    @pl.when(j == 0)
    def _():
        acc_scr[...] = partial()
        emit_norm()

    @pl.when((j > 0) & (j < last))
    def _():
        acc_scr[...] += partial()

    @pl.when(j == last)
    def _():
        pend_scr[...] = ALPHA * xf_ref[...] + (acc_scr[...] + partial())

    @pl.when((j == last) & (i == pl.num_programs(0) - 1))
    def _():
        emit_norm()


def _merge_out(xb, xf, ya, yb, w_in, w_a, w_b, w_o, g, b):
    m, d = xf.shape
    nc = d // MERGE_TC
    ga0, gb0 = COL_GA // MERGE_TC, COL_GB // MERGE_TC
    row = lambda i, j: (i, 0)
    return pl.pallas_call(
        _merge_kernel,
        grid=(m // MERGE_TM, nc),
        in_specs=[
            pl.BlockSpec((MERGE_TM, d), row),
            pl.BlockSpec((MERGE_TM, d), row),
            pl.BlockSpec((MERGE_TM, SGU_WIDTH), row),
            pl.BlockSpec((MERGE_TM, ATTN_WIDTH), row),
            pl.BlockSpec((d, MERGE_TC), lambda i, j: (0, ga0 + j)),
            pl.BlockSpec((d, MERGE_TC), lambda i, j: (0, gb0 + j)),
            pl.BlockSpec((SGU_WIDTH, MERGE_TC), lambda i, j: (0, j)),
            pl.BlockSpec((ATTN_WIDTH, MERGE_TC), lambda i, j: (0, j)),
            pl.BlockSpec((MERGE_TC, d), lambda i, j: (j, 0)),
            pl.BlockSpec((1, d), lambda i, j: (0, 0)),
            pl.BlockSpec((1, d), lambda i, j: (0, 0)),
        ],
        out_specs=pl.BlockSpec((MERGE_TM, d), _deferred_out_index),
        out_shape=jax.ShapeDtypeStruct((m, d), F32),
        scratch_shapes=[pltpu.VMEM((MERGE_TM, d), F32), pltpu.VMEM((MERGE_TM, d), F32)],
        compiler_params=_params(("arbitrary", "arbitrary")),
        name="merge_out",
    )(xb, xf, ya, yb, w_in, w_in, w_a, w_b, w_o, g, b)


PE_TM = 512
PE_TN = 512


def _pe_kernel(xb_ref, xf_ref, p_ref, wg_ref, wp_ref, g_ref, b_ref, o_ref, y_scr, pend_scr):
    i, j = pl.program_id(0), pl.program_id(1)
    last = pl.num_programs(1) - 1
    c_last = y_scr.shape[1] - PE_TN

    def chunk(c0):
        gate = jax.nn.sigmoid(_dot(xb_ref[...], wg_ref[...]))
        emb = _dot(p_ref[...].astype(BF16), wp_ref[...])
        return ALPHA * xf_ref[:, pl.ds(c0, PE_TN)] + gate * emb

    def emit_norm():
        o_ref[...] = _layer_norm(pend_scr[...], g_ref[...], b_ref[...])

    @pl.when((i == 0) & (j == 0))
    def _():
        pend_scr[...] = jnp.zeros_like(pend_scr)

    @pl.when(j == 0)
    def _():
        y_scr[:, :PE_TN] = chunk(0)
        emit_norm()

    @pl.when((j > 0) & (j < last))
    def _():
        c0 = pl.multiple_of(j * PE_TN, PE_TN)
        y_scr[:, pl.ds(c0, PE_TN)] = chunk(c0)

    @pl.when(j == last)
    def _():
        pend_scr[:, c_last:] = chunk(c_last)
        pend_scr[:, :c_last] = y_scr[:, :c_last]

    @pl.when((j == last) & (i == pl.num_programs(0) - 1))
    def _():
        emit_norm()


def _pe_gate(xb, xf, p, w_g, w_p, g, b):
    m, d = xf.shape
    row = lambda i, j: (i, 0)
    return pl.pallas_call(
        _pe_kernel,
        grid=(m // PE_TM, d // PE_TN),
        in_specs=[
            pl.BlockSpec((PE_TM, d), row),
            pl.BlockSpec((PE_TM, d), row),
            pl.BlockSpec((PE_TM, P_DIM), row),
            pl.BlockSpec((d, PE_TN), lambda i, j: (0, j)),
            pl.BlockSpec((P_DIM, PE_TN), lambda i, j: (0, j)),
            pl.BlockSpec((1, d), lambda i, j: (0, 0)),
            pl.BlockSpec((1, d), lambda i, j: (0, 0)),
        ],
        out_specs=pl.BlockSpec((PE_TM, d), _deferred_out_index),
        out_shape=jax.ShapeDtypeStruct((m, d), F32),
        scratch_shapes=[pltpu.VMEM((PE_TM, d), F32), pltpu.VMEM((PE_TM, d), F32)],
        compiler_params=_params(("arbitrary", "arbitrary")),
        name="pe_gate",
    )(xb, xf, p, w_g, w_p, g, b)


def kernel(x, p, ffn1_w_gu, ffn1_w_down, ln1_g, ln1_b, w_in, sgu_ln_g, sgu_ln_b, sgu_w, sgu_b, lam_q1, lam_k1, lam_q2, lam_k2, attn_norm_g, w_branch_a, w_branch_b, w_out, ln2_g, ln2_b, ffn2_w_gu, ffn2_w_down, ln3_g, ln3_b, w_pe_gate, w_pe_proj, ln4_g, ln4_b):
    batch, seq, d = x.shape
    m = batch * seq
    slopes = jnp.asarray(2.0 ** (-8.0 * np.arange(1, N_HEADS + 1) / N_HEADS), dtype=F32)
    xf = x.reshape(m, d)
    for i in range(DEPTH):
        w_in_b = w_in[i].astype(BF16)
        bias_full = jnp.repeat(sgu_b[i].T, SGU_BLOCK, axis=1)
        lam_vecs = jnp.stack([lam_q1[i], lam_k1[i], lam_q2[i], lam_k2[i]]).astype(F32)

        x1f, x1b = _ffn_ln(xf, ffn1_w_gu[i].astype(BF16), ffn1_w_down[i].astype(BF16),
                           ln1_g[i][None], ln1_b[i][None])
        mix = _in_proj(x1b, w_in_b, sgu_ln_g[i][None], sgu_ln_b[i][None])
        y_a = _sgu(mix, sgu_w[i], bias_full)
        y_b = _diff_attn(mix, slopes, lam_vecs, attn_norm_g[i][None], batch, seq)
        x2f = _merge_out(x1b, x1f, y_a, y_b, w_in_b, w_branch_a[i].astype(BF16), w_branch_b[i].astype(BF16),
                            w_out[i].astype(BF16), ln2_g[i][None], ln2_b[i][None])
        x3f, x3b = _ffn_ln(x2f, ffn2_w_gu[i].astype(BF16), ffn2_w_down[i].astype(BF16),
                           ln3_g[i][None], ln3_b[i][None])
        xf = _pe_gate(x3b, x3f, p[i].reshape(m, P_DIM), w_pe_gate[i].astype(BF16), w_pe_proj[i].astype(BF16),
                      ln4_g[i][None], ln4_b[i][None])
    return xf.reshape(batch, seq, d)
```

```python
import math

import jax
import jax.numpy as jnp
import numpy as np
from jax import lax
from jax.experimental import pallas as pl
from jax.experimental.pallas import tpu as pltpu

CHUNK = 64
P_DIM = 256
SGU_BLOCK = 128
SGU_GROUPS = 8
SGU_WIDTH = 1024
N_HEADS = 8
HEAD_DIM = 64
V_HEAD_DIM = 128
ATTN_WIDTH = 1024
DEPTH = 1
ALPHA = (2 * DEPTH) ** 0.25
LN_EPS = 1e-5
LAM_INIT = 0.8 - 0.6 * math.exp(-0.3 * 0)

COL_U, COL_V, COL_Q, COL_K, COL_VAL = 0, 1024, 2048, 3072, 4096
COL_GA = 5120
MIX_COLS = 5120

BF16 = jnp.bfloat16
F32 = jnp.float32

VMEM_LIMIT_BYTES = 56 * 1024 * 1024

MASK_VALUE = -1e30


def _params(semantics):
    return pltpu.CompilerParams(dimension_semantics=semantics, vmem_limit_bytes=VMEM_LIMIT_BYTES)


def _dot(a, b):
    return jnp.dot(a, b, preferred_element_type=F32)


def _chunked(w, groups, width):
    k, n = w.shape[0], w.shape[1] // groups
    w = w.reshape(k, groups, n // width, width).transpose(2, 0, 1, 3)
    return w.reshape(n // width, k, groups * width).astype(BF16)


def _layer_norm(y, g, b):
    mu = jnp.mean(y, axis=-1, keepdims=True)
    d = y - mu
    var = jnp.mean(d * d, axis=-1, keepdims=True)
    return d * lax.rsqrt(var + LN_EPS) * g + b


FFN_TM = 512
FFN_TF = 512


def _deferred_out_index(i, j):
    return (jnp.where(j == 0, jnp.maximum(i - 1, 0), i), 0)


def _ffn_ln_kernel(x_ref, wgu_ref, wd_ref, g_ref, b_ref, of_ref, ob_ref, xb_scr, acc_scr, pend_scr):
    i, j = pl.program_id(0), pl.program_id(1)
    last = pl.num_programs(1) - 1

    def partial():
        gu = _dot(xb_scr[...], wgu_ref[...])
        gate, up = gu[:, :FFN_TF], gu[:, FFN_TF:]
        h = (gate * jax.nn.sigmoid(gate)) * up
        return _dot(h.astype(BF16), wd_ref[...])

    def emit_norm():
        out = _layer_norm(pend_scr[...], g_ref[...], b_ref[...])
        of_ref[...] = out
        ob_ref[...] = out.astype(BF16)

    @pl.when((i == 0) & (j == 0))
    def _():
        pend_scr[...] = jnp.zeros_like(pend_scr)

    @pl.when(j == 0)
    def _():
        xb_scr[...] = x_ref[...].astype(BF16)
        acc_scr[...] = partial()
        emit_norm()

    @pl.when((j > 0) & (j < last))
    def _():
        acc_scr[...] += partial()

    @pl.when(j == last)
    def _():
        pend_scr[...] = ALPHA * x_ref[...] + 0.5 * (acc_scr[...] + partial())

    @pl.when((j == last) & (i == pl.num_programs(0) - 1))
    def _():
        emit_norm()


def _ffn_ln(x, w_gu, w_down, g, b):
    m, d = x.shape
    nf = w_gu.shape[0]
    return pl.pallas_call(
        _ffn_ln_kernel,
        grid=(m // FFN_TM, nf),
        in_specs=[
            pl.BlockSpec((FFN_TM, d), lambda i, j: (i, 0)),
            pl.BlockSpec((None, d, 2 * FFN_TF), lambda i, j: (j, 0, 0)),
            pl.BlockSpec((FFN_TF, d), lambda i, j: (j, 0)),
            pl.BlockSpec((1, d), lambda i, j: (0, 0)),
            pl.BlockSpec((1, d), lambda i, j: (0, 0)),
        ],
        out_specs=[
            pl.BlockSpec((FFN_TM, d), _deferred_out_index),
            pl.BlockSpec((FFN_TM, d), _deferred_out_index),
        ],
        out_shape=[jax.ShapeDtypeStruct((m, d), F32), jax.ShapeDtypeStruct((m, d), BF16)],
        scratch_shapes=[pltpu.VMEM((FFN_TM, d), BF16), pltpu.VMEM((FFN_TM, d), F32),
                        pltpu.VMEM((FFN_TM, d), F32)],
        compiler_params=_params(("arbitrary", "arbitrary")),
        name="ffn_ln",
    )(x, w_gu, w_down, g, b)


PROJ_TM = 512
PROJ_TN = 1024


def _in_proj_kernel(x_ref, w_ref, g_ref, b_ref, o_ref):
    j = pl.program_id(1)

    def proj():
        return _dot(x_ref[...], w_ref[...])

    @pl.when(j == COL_U // PROJ_TN)
    def _():
        o_ref[...] = jax.nn.gelu(proj()).astype(BF16)

    @pl.when(j == COL_V // PROJ_TN)
    def _():
        o_ref[...] = _layer_norm(jax.nn.gelu(proj()), g_ref[...], b_ref[...]).astype(BF16)

    @pl.when(j == COL_Q // PROJ_TN)
    def _():
        o_ref[...] = (proj() * HEAD_DIM ** -0.5).astype(BF16)

    @pl.when(j >= COL_K // PROJ_TN)
    def _():
        o_ref[...] = proj().astype(BF16)


def _in_proj(xb, w_mix, sgu_g, sgu_b):
    m, d = xb.shape
    return pl.pallas_call(
        _in_proj_kernel,
        grid=(m // PROJ_TM, MIX_COLS // PROJ_TN),
        in_specs=[
            pl.BlockSpec((PROJ_TM, d), lambda i, j: (i, 0)),
            pl.BlockSpec((None, d, PROJ_TN), lambda i, j: (j, 0, 0)),
            pl.BlockSpec((1, SGU_WIDTH), lambda i, j: (0, 0)),
            pl.BlockSpec((1, SGU_WIDTH), lambda i, j: (0, 0)),
        ],
        out_specs=pl.BlockSpec((PROJ_TM, PROJ_TN), lambda i, j: (i, j)),
        out_shape=jax.ShapeDtypeStruct((m, MIX_COLS), BF16),
        compiler_params=_params(("parallel", "arbitrary")),
        name="in_proj",
    )(xb, w_mix, sgu_g, sgu_b)


SGU_TM = 512


def _sgu_kernel(u_ref, v_ref, w_ref, bias_ref, o_ref):
    t = lax.broadcasted_iota(jnp.int32, (SGU_BLOCK, SGU_BLOCK), 0)
    s = lax.broadcasted_iota(jnp.int32, (SGU_BLOCK, SGU_BLOCK), 1)
    allowed = (s // CHUNK) <= (t // CHUNK)
    for g in range(SGU_GROUPS):
        wm = jnp.where(allowed, w_ref[g], 0.0).astype(BF16)
        cols = slice(g * SGU_BLOCK, (g + 1) * SGU_BLOCK)
        bias = bias_ref[:, cols]
        for blk in range(SGU_TM // SGU_BLOCK):
            rows = slice(blk * SGU_BLOCK, (blk + 1) * SGU_BLOCK)
            mix = _dot(wm, v_ref[rows, cols]) + bias
            o_ref[rows, cols] = (u_ref[rows, cols].astype(F32) * mix).astype(BF16)


def _sgu(mix, sgu_w, bias_full):
    m = mix.shape[0]
    return pl.pallas_call(
        _sgu_kernel,
        grid=(m // SGU_TM,),
        in_specs=[
            pl.BlockSpec((SGU_TM, SGU_WIDTH), lambda i: (i, COL_U // SGU_WIDTH)),
            pl.BlockSpec((SGU_TM, SGU_WIDTH), lambda i: (i, COL_V // SGU_WIDTH)),
            pl.BlockSpec((SGU_GROUPS, SGU_BLOCK, SGU_BLOCK), lambda i: (0, 0, 0)),
            pl.BlockSpec((SGU_BLOCK, SGU_WIDTH), lambda i: (0, 0)),
        ],
        out_specs=pl.BlockSpec((SGU_TM, SGU_WIDTH), lambda i: (i, 0)),
        out_shape=jax.ShapeDtypeStruct((m, SGU_WIDTH), BF16),
        compiler_params=_params(("parallel",)),
        name="sgu",
    )(mix, mix, sgu_w, bias_full)


ATT_TK = 512
ATT_TQ = 2 * ATT_TK
ATT_ROWS = 2 * ATT_TQ
AUG = 128


def _lane_tile(x, n):
    return jnp.concatenate([x] * n, axis=1)


def _attn_kernel(slopes_ref, lam_ref, q_ref, k_ref, v_ref, g_ref, o_ref,
                 kaug_scr, vaug_scr, diag_scr, qa_scr, sa_scr, sb_scr, m_scr, acc_scr):
    h = pl.program_id(1)
    qi = pl.program_id(2)
    tq, tk, rows = ATT_TQ, ATT_TK, ATT_ROWS
    seq = k_ref.shape[0]
    slope = slopes_ref[h]
    q0 = pl.multiple_of(qi * tq, tq)

    @pl.when(qi == 0)
    def _():
        pos = lax.broadcasted_iota(jnp.int32, (seq, AUG), 0)
        lane = lax.broadcasted_iota(jnp.int32, (seq, AUG), 1)
        feat = jnp.where(lane == 0, pos // CHUNK, jnp.where(lane == 1, pos % CHUNK, jnp.where(lane == 2, 1, 0)))
        kaug_scr[:, :V_HEAD_DIM] = k_ref[...]
        kaug_scr[:, V_HEAD_DIM:] = feat.astype(F32).astype(BF16)
        vaug_scr[:, :V_HEAD_DIM] = v_ref[...]
        vaug_scr[:, V_HEAD_DIM:] = jnp.ones((seq, AUG), BF16)
        t = lax.broadcasted_iota(jnp.int32, (tk, tk), 0)
        c = lax.broadcasted_iota(jnp.int32, (tk, tk), 1)
        ahead = jnp.maximum(c - t, 0).astype(F32)
        diag_scr[...] = jnp.where(c // CHUNK <= t // CHUNK, (-2.0 * slope) * ahead, MASK_VALUE)

    lane = lax.broadcasted_iota(jnp.int32, (tk, V_HEAD_DIM), 1)
    aug = jnp.where(lane == 0, CHUNK * slope,
                    jnp.where(lane == 1, slope, jnp.where(lane == 2, -slope * q0.astype(F32), 0.0))).astype(BF16)
    for half in range(2):
        q = q_ref[half * tk:(half + 1) * tk, :]
        zero = jnp.zeros_like(q)
        r1, r2 = 2 * half * tk, (2 * half + 1) * tk
        qa_scr[r1:r1 + tk, :V_HEAD_DIM] = jnp.where(lane < HEAD_DIM, q, zero)
        qa_scr[r2:r2 + tk, :V_HEAD_DIM] = jnp.where(lane >= HEAD_DIM, q, zero)
        qa_scr[r1:r1 + tk, V_HEAD_DIM:] = aug
        qa_scr[r2:r2 + tk, V_HEAD_DIM:] = aug
    m_scr[...] = jnp.full_like(m_scr, MASK_VALUE)
    acc_scr[...] = jnp.zeros_like(acc_scr)

    def scores(k0, r0=0):
        return lax.dot_general(qa_scr[r0:, :], kaug_scr[pl.ds(k0, tk), :], (((1,), (1,)), ((), ())),
                               preferred_element_type=F32)

    def update(s, k0, rs=slice(None)):
        width = s.shape[1]
        m_prev = m_scr[rs, :]
        m_new = jnp.maximum(m_prev, jnp.max(s, axis=1, keepdims=True))
        alpha = jnp.exp(m_prev - m_new)
        p = jnp.exp(s - _lane_tile(m_new, width // 128))
        m_scr[rs, :] = m_new
        acc_scr[rs, :] = (acc_scr[rs, :] * _lane_tile(alpha, 2)
                          + _dot(p.astype(BF16), vaug_scr[pl.ds(k0, width), :]))

    sa_scr[...] = scores(0)

    def body(pair, carry):
        k0 = pl.multiple_of(pair * (2 * tk), 2 * tk)
        sb_scr[...] = scores(k0 + tk)
        update(sa_scr[...], k0)
        sa_scr[...] = scores(k0 + 2 * tk)
        update(sb_scr[...], k0 + tk)
        return carry

    lax.fori_loop(0, qi, body, 0)

    first, second = slice(0, 2 * tk), slice(2 * tk, rows)
    diag2 = jnp.concatenate([diag_scr[...], diag_scr[...]], axis=0)
    sb_scr[second, :] = scores(q0 + tk, 2 * tk)
    update(sa_scr[first, :] + diag2, q0, first)
    update(jnp.concatenate([sa_scr[second, :], sb_scr[second, :] + diag2], axis=1), q0, second)

    lv = lam_ref[...]
    lam = (jnp.exp(jnp.sum(lv[0:1] * lv[1:2], axis=-1, keepdims=True))
           - jnp.exp(jnp.sum(lv[2:3] * lv[3:4], axis=-1, keepdims=True)) + LAM_INIT)
    for half in range(2):
        r1, r2 = 2 * half * tk, (2 * half + 1) * tk
        a1, a2 = acc_scr[r1:r1 + tk, :], acc_scr[r2:r2 + tk, :]
        o = a1[:, :V_HEAD_DIM] / a1[:, V_HEAD_DIM:] - lam * (a2[:, :V_HEAD_DIM] / a2[:, V_HEAD_DIM:])
        y = o * lax.rsqrt(jnp.mean(o * o, axis=-1, keepdims=True) + LN_EPS)
        o_ref[half * tk:(half + 1) * tk, :] = (y * g_ref[...] * (1.0 - LAM_INIT)).astype(BF16)


def _diff_attn(mix, slopes, lam_vecs, attn_g, batch, seq):
    m = mix.shape[0]
    nq = seq // ATT_TQ
    qc, kc, vc = COL_Q // V_HEAD_DIM, COL_K // V_HEAD_DIM, COL_VAL // V_HEAD_DIM
    return pl.pallas_call(
        _attn_kernel,
        grid=(batch, N_HEADS, nq),
        in_specs=[
            pl.BlockSpec(memory_space=pltpu.SMEM),
            pl.BlockSpec((4, HEAD_DIM), lambda b, h, i: (0, 0)),
            pl.BlockSpec((ATT_TQ, V_HEAD_DIM), lambda b, h, i: (b * nq + i, qc + h)),
            pl.BlockSpec((seq, V_HEAD_DIM), lambda b, h, i: (b, kc + h)),
            pl.BlockSpec((seq, V_HEAD_DIM), lambda b, h, i: (b, vc + h)),
            pl.BlockSpec((1, V_HEAD_DIM), lambda b, h, i: (0, h)),
        ],
        out_specs=pl.BlockSpec((ATT_TQ, V_HEAD_DIM), lambda b, h, i: (b * nq + i, h)),
        out_shape=jax.ShapeDtypeStruct((m, ATTN_WIDTH), BF16),
        scratch_shapes=[
            pltpu.VMEM((seq, V_HEAD_DIM + AUG), BF16),
            pltpu.VMEM((seq, V_HEAD_DIM + AUG), BF16),
            pltpu.VMEM((ATT_TK, ATT_TK), F32),
            pltpu.VMEM((ATT_ROWS, V_HEAD_DIM + AUG), BF16),
            pltpu.VMEM((ATT_ROWS, ATT_TK), F32),
            pltpu.VMEM((ATT_ROWS, ATT_TK), F32),
            pltpu.VMEM((ATT_ROWS, 128), F32),
            pltpu.VMEM((ATT_ROWS, V_HEAD_DIM + AUG), F32),
        ],
        compiler_params=_params(("parallel", "parallel", "arbitrary")),
        name="diff_attn",
    )(slopes, lam_vecs, mix, mix, mix, attn_g)


MERGE_TM = 512
MERGE_TC = 512


def _merge_kernel(xb_ref, xf_ref, ya_ref, yb_ref, wg_ref, wab_ref, wo_ref, g_ref, b_ref,
                  o_ref, acc_scr, pend_scr):
    i, j = pl.program_id(0), pl.program_id(1)
    last = pl.num_programs(1) - 1

    def partial():
        gates = jax.nn.sigmoid(_dot(xb_ref[...], wg_ref[...]))
        merged = (gates[:, :MERGE_TC] * _dot(ya_ref[...], wab_ref[0])
                  + gates[:, MERGE_TC:] * _dot(yb_ref[...], wab_ref[1]))
        return _dot(merged.astype(BF16), wo_ref[...])

    def emit_norm():
        o_ref[...] = _layer_norm(pend_scr[...], g_ref[...], b_ref[...])

    @pl.when((i == 0) & (j == 0))
    def _():
        pend_scr[...] = jnp.zeros_like(pend_scr)

    @pl.when(j == 0)
    def _():
        acc_scr[...] = partial()
        emit_norm()

    @pl.when((j > 0) & (j < last))
    def _():
        acc_scr[...] += partial()

    @pl.when(j == last)
    def _():
        pend_scr[...] = ALPHA * xf_ref[...] + (acc_scr[...] + partial())

    @pl.when((j == last) & (i == pl.num_programs(0) - 1))
    def _():
        emit_norm()


def _merge_out(xb, xf, ya, yb, w_gates, w_ab, w_o, g, b):
    m, d = xf.shape
    nc = d // MERGE_TC
    row = lambda i, j: (i, 0)
    return pl.pallas_call(
        _merge_kernel,
        grid=(m // MERGE_TM, nc),
        in_specs=[
            pl.BlockSpec((MERGE_TM, d), row),
            pl.BlockSpec((MERGE_TM, d), row),
            pl.BlockSpec((MERGE_TM, SGU_WIDTH), row),
            pl.BlockSpec((MERGE_TM, ATTN_WIDTH), row),
            pl.BlockSpec((None, d, 2 * MERGE_TC), lambda i, j: (j, 0, 0)),
            pl.BlockSpec((None, 2, SGU_WIDTH, MERGE_TC), lambda i, j: (j, 0, 0, 0)),
            pl.BlockSpec((MERGE_TC, d), lambda i, j: (j, 0)),
            pl.BlockSpec((1, d), lambda i, j: (0, 0)),
            pl.BlockSpec((1, d), lambda i, j: (0, 0)),
        ],
        out_specs=pl.BlockSpec((MERGE_TM, d), _deferred_out_index),
        out_shape=jax.ShapeDtypeStruct((m, d), F32),
        scratch_shapes=[pltpu.VMEM((MERGE_TM, d), F32), pltpu.VMEM((MERGE_TM, d), F32)],
        compiler_params=_params(("arbitrary", "arbitrary")),
        name="merge_out",
    )(xb, xf, ya, yb, w_gates, w_ab, w_o, g, b)


PE_TM = 512
PE_TN = 512


def _pe_kernel(xb_ref, xf_ref, p_ref, wg_ref, wp_ref, g_ref, b_ref, o_ref, y_scr, pend_scr):
    i, j = pl.program_id(0), pl.program_id(1)
    last = pl.num_programs(1) - 1
    c_last = y_scr.shape[1] - PE_TN

    def chunk(c0):
        gate = jax.nn.sigmoid(_dot(xb_ref[...], wg_ref[...]))
        emb = _dot(p_ref[...].astype(BF16), wp_ref[...])
        return ALPHA * xf_ref[:, pl.ds(c0, PE_TN)] + gate * emb

    def emit_norm():
        o_ref[...] = _layer_norm(pend_scr[...], g_ref[...], b_ref[...])

    @pl.when((i == 0) & (j == 0))
    def _():
        pend_scr[...] = jnp.zeros_like(pend_scr)

    @pl.when(j == 0)
    def _():
        y_scr[:, :PE_TN] = chunk(0)
        emit_norm()

    @pl.when((j > 0) & (j < last))
    def _():
        c0 = pl.multiple_of(j * PE_TN, PE_TN)
        y_scr[:, pl.ds(c0, PE_TN)] = chunk(c0)

    @pl.when(j == last)
    def _():
        pend_scr[:, c_last:] = chunk(c_last)
        pend_scr[:, :c_last] = y_scr[:, :c_last]

    @pl.when((j == last) & (i == pl.num_programs(0) - 1))
    def _():
        emit_norm()


def _pe_gate(xb, xf, p, w_g, w_p, g, b):
    m, d = xf.shape
    row = lambda i, j: (i, 0)
    return pl.pallas_call(
        _pe_kernel,
        grid=(m // PE_TM, d // PE_TN),
        in_specs=[
            pl.BlockSpec((PE_TM, d), row),
            pl.BlockSpec((PE_TM, d), row),
            pl.BlockSpec((PE_TM, P_DIM), row),
            pl.BlockSpec((None, d, PE_TN), lambda i, j: (j, 0, 0)),
            pl.BlockSpec((None, P_DIM, PE_TN), lambda i, j: (j, 0, 0)),
            pl.BlockSpec((1, d), lambda i, j: (0, 0)),
            pl.BlockSpec((1, d), lambda i, j: (0, 0)),
        ],
        out_specs=pl.BlockSpec((PE_TM, d), _deferred_out_index),
        out_shape=jax.ShapeDtypeStruct((m, d), F32),
        scratch_shapes=[pltpu.VMEM((PE_TM, d), F32), pltpu.VMEM((PE_TM, d), F32)],
        compiler_params=_params(("arbitrary", "arbitrary")),
        name="pe_gate",
    )(xb, xf, p, w_g, w_p, g, b)


def kernel(x, p, ffn1_w_gu, ffn1_w_down, ln1_g, ln1_b, w_in, sgu_ln_g, sgu_ln_b, sgu_w, sgu_b, lam_q1, lam_k1, lam_q2, lam_k2, attn_norm_g, w_branch_a, w_branch_b, w_out, ln2_g, ln2_b, ffn2_w_gu, ffn2_w_down, ln3_g, ln3_b, w_pe_gate, w_pe_proj, ln4_g, ln4_b):
    batch, seq, d = x.shape
    m = batch * seq
    slopes = jnp.asarray(2.0 ** (-8.0 * np.arange(1, N_HEADS + 1) / N_HEADS), dtype=F32)
    xf = x.reshape(m, d)
    for i in range(DEPTH):
        w_mix = _chunked(w_in[i][:, :MIX_COLS], 1, PROJ_TN)
        w_gates = _chunked(w_in[i][:, COL_GA:], 2, MERGE_TC)
        w_ab = jnp.stack([_chunked(w_branch_a[i], 1, MERGE_TC), _chunked(w_branch_b[i], 1, MERGE_TC)], axis=1)
        bias_full = jnp.repeat(sgu_b[i].T, SGU_BLOCK, axis=1)
        lam_vecs = jnp.stack([lam_q1[i], lam_k1[i], lam_q2[i], lam_k2[i]]).astype(F32)

        x1f, x1b = _ffn_ln(xf, _chunked(ffn1_w_gu[i], 2, FFN_TF), ffn1_w_down[i].astype(BF16),
                           ln1_g[i][None], ln1_b[i][None])
        mix = _in_proj(x1b, w_mix, sgu_ln_g[i][None], sgu_ln_b[i][None])
        y_a = _sgu(mix, sgu_w[i], bias_full)
        y_b = _diff_attn(mix, slopes, lam_vecs, attn_norm_g[i][None], batch, seq)
        x2f = _merge_out(x1b, x1f, y_a, y_b, w_gates, w_ab, w_out[i].astype(BF16), ln2_g[i][None], ln2_b[i][None])
        x3f, x3b = _ffn_ln(x2f, _chunked(ffn2_w_gu[i], 2, FFN_TF), ffn2_w_down[i].astype(BF16),
                           ln3_g[i][None], ln3_b[i][None])
        xf = _pe_gate(x3b, x3f, p[i].reshape(m, P_DIM), _chunked(w_pe_gate[i], 1, PE_TN),
                      _chunked(w_pe_proj[i], 1, PE_TN), ln4_g[i][None], ln4_b[i][None])
    return xf.reshape(batch, seq, d)
```

```python
import math

import jax
import jax.numpy as jnp
import numpy as np
from jax import lax
from jax.experimental import pallas as pl
from jax.experimental.pallas import tpu as pltpu

CHUNK = 64
P_DIM = 256
SGU_BLOCK = 128
SGU_GROUPS = 8
SGU_WIDTH = 1024
N_HEADS = 8
HEAD_DIM = 64
V_HEAD_DIM = 128
ATTN_WIDTH = 1024
DEPTH = 1
ALPHA = (2 * DEPTH) ** 0.25
LN_EPS = 1e-5
LAM_INIT = 0.8 - 0.6 * math.exp(-0.3 * 0)

COL_U, COL_V, COL_Q, COL_K, COL_VAL = 0, 1024, 2048, 3072, 4096
COL_GA, COL_GB = 5120, 7168
MIX_COLS = 5120

BF16 = jnp.bfloat16
F32 = jnp.float32

VMEM_LIMIT_BYTES = 56 * 1024 * 1024

MASK_VALUE = -1e30


def _params(semantics):
    return pltpu.CompilerParams(dimension_semantics=semantics, vmem_limit_bytes=VMEM_LIMIT_BYTES)


def _dot(a, b):
    return jnp.dot(a, b, preferred_element_type=F32)


def _stream_blocks(streams, step, total):
    def copies(s, slot):
        return [pltpu.make_async_copy(block_at(s), buf.at[slot], sem.at[slot]) for block_at, buf, sem in streams]

    slot = lax.rem(step, 2)

    @pl.when(step == 0)
    def _():
        for c in copies(step, slot):
            c.start(priority=1)

    @pl.when(step + 1 < total)
    def _():
        for c in copies(step + 1, 1 - slot):
            c.start(priority=1)

    for c in copies(step, slot):
        c.wait()
    return slot


def _linear_step():
    nj = pl.num_programs(1)
    return pl.program_id(0) * nj + pl.program_id(1), pl.num_programs(0) * nj


def _layer_norm(y, g, b):
    mu = jnp.mean(y, axis=-1, keepdims=True)
    d = y - mu
    var = jnp.mean(d * d, axis=-1, keepdims=True)
    return d * lax.rsqrt(var + LN_EPS) * g + b


FFN_TM = 512
FFN_TF = 512


def _deferred_out_index(i, j):
    return (jnp.where(j == 0, jnp.maximum(i - 1, 0), i), 0)


def _ffn_ln_kernel(x_ref, wg_ref, wu_ref, wd_hbm, g_ref, b_ref, of_ref, ob_ref,
                   xb_scr, acc_scr, pend_scr, wd_buf, wd_sem):
    i, j = pl.program_id(0), pl.program_id(1)
    nf = pl.num_programs(1)
    last = nf - 1
    step, total = _linear_step()

    def wd_block(s):
        return wd_hbm.at[pl.ds(pl.multiple_of(lax.rem(s, nf) * FFN_TF, FFN_TF), FFN_TF), :]

    slot = _stream_blocks([(wd_block, wd_buf, wd_sem)], step, total)

    def partial():
        xb = xb_scr[...]
        gate = _dot(xb, wg_ref[...])
        up = _dot(xb, wu_ref[...])
        h = (gate * jax.nn.sigmoid(gate)) * up
        return _dot(h.astype(BF16), wd_buf[slot])

    def emit_norm():
        out = _layer_norm(pend_scr[...], g_ref[...], b_ref[...])
        of_ref[...] = out
        ob_ref[...] = out.astype(BF16)

    @pl.when((i == 0) & (j == 0))
    def _():
        pend_scr[...] = jnp.zeros_like(pend_scr)

    @pl.when(j == 0)
    def _():
        xb_scr[...] = x_ref[...].astype(BF16)
        acc_scr[...] = partial()
        emit_norm()

    @pl.when((j > 0) & (j < last))
    def _():
        acc_scr[...] += partial()

    @pl.when(j == last)
    def _():
        pend_scr[...] = ALPHA * x_ref[...] + 0.5 * (acc_scr[...] + partial())

    @pl.when((j == last) & (i == pl.num_programs(0) - 1))
    def _():
        emit_norm()


def _ffn_ln(x, w_gu, w_down, g, b):
    m, d = x.shape
    f = w_down.shape[0]
    nf = f // FFN_TF
    return pl.pallas_call(
        _ffn_ln_kernel,
        grid=(m // FFN_TM, nf),
        in_specs=[
            pl.BlockSpec((FFN_TM, d), lambda i, j: (i, 0)),
            pl.BlockSpec((d, FFN_TF), lambda i, j: (0, j)),
            pl.BlockSpec((d, FFN_TF), lambda i, j: (0, nf + j)),
            pl.BlockSpec(memory_space=pl.ANY),
            pl.BlockSpec((1, d), lambda i, j: (0, 0)),
            pl.BlockSpec((1, d), lambda i, j: (0, 0)),
        ],
        out_specs=[
            pl.BlockSpec((FFN_TM, d), _deferred_out_index),
            pl.BlockSpec((FFN_TM, d), _deferred_out_index),
        ],
        out_shape=[jax.ShapeDtypeStruct((m, d), F32), jax.ShapeDtypeStruct((m, d), BF16)],
        scratch_shapes=[pltpu.VMEM((FFN_TM, d), BF16), pltpu.VMEM((FFN_TM, d), F32),
                        pltpu.VMEM((FFN_TM, d), F32),
                        pltpu.VMEM((2, FFN_TF, d), BF16), pltpu.SemaphoreType.DMA((2,))],
        compiler_params=_params(("arbitrary", "arbitrary")),
        name="ffn_ln",
    )(x, w_gu, w_gu, w_down, g, b)


PROJ_TM = 512
PROJ_TN = 1024


def _in_proj_kernel(x_ref, w0_ref, w1_hbm, g_ref, b_ref, o_ref, w1_buf, w1_sem):
    j = pl.program_id(1)
    nj = pl.num_programs(1)
    step, total = _linear_step()
    half = PROJ_TN // 2

    def w1_block(s):
        return w1_hbm.at[:, pl.ds(pl.multiple_of(lax.rem(s, nj) * PROJ_TN + half, half), half)]

    slot = _stream_blocks([(w1_block, w1_buf, w1_sem)], step, total)

    def proj():
        x = x_ref[...]
        return jnp.concatenate([_dot(x, w0_ref[...]), _dot(x, w1_buf[slot])], axis=1)

    @pl.when(j == COL_U // PROJ_TN)
    def _():
        o_ref[...] = jax.nn.gelu(proj()).astype(BF16)

    @pl.when(j == COL_V // PROJ_TN)
    def _():
        o_ref[...] = _layer_norm(jax.nn.gelu(proj()), g_ref[...], b_ref[...]).astype(BF16)

    @pl.when(j == COL_Q // PROJ_TN)
    def _():
        o_ref[...] = (proj() * HEAD_DIM ** -0.5).astype(BF16)

    @pl.when(j >= COL_K // PROJ_TN)
    def _():
        o_ref[...] = proj().astype(BF16)


def _in_proj(xb, w_in, sgu_g, sgu_b):
    m, d = xb.shape
    return pl.pallas_call(
        _in_proj_kernel,
        grid=(m // PROJ_TM, MIX_COLS // PROJ_TN),
        in_specs=[
            pl.BlockSpec((PROJ_TM, d), lambda i, j: (i, 0)),
            pl.BlockSpec((d, PROJ_TN // 2), lambda i, j: (0, 2 * j)),
            pl.BlockSpec(memory_space=pl.ANY),
            pl.BlockSpec((1, SGU_WIDTH), lambda i, j: (0, 0)),
            pl.BlockSpec((1, SGU_WIDTH), lambda i, j: (0, 0)),
        ],
        out_specs=pl.BlockSpec((PROJ_TM, PROJ_TN), lambda i, j: (i, j)),
        out_shape=jax.ShapeDtypeStruct((m, MIX_COLS), BF16),
        scratch_shapes=[pltpu.VMEM((2, d, PROJ_TN // 2), BF16), pltpu.SemaphoreType.DMA((2,))],
        compiler_params=_params(("arbitrary", "arbitrary")),
        name="in_proj",
    )(xb, w_in, w_in, sgu_g, sgu_b)


SGU_TM = 512


def _sgu_kernel(u_ref, v_ref, w_ref, bias_ref, o_ref):
    t = lax.broadcasted_iota(jnp.int32, (SGU_BLOCK, SGU_BLOCK), 0)
    s = lax.broadcasted_iota(jnp.int32, (SGU_BLOCK, SGU_BLOCK), 1)
    allowed = (s // CHUNK) <= (t // CHUNK)
    for g in range(SGU_GROUPS):
        wm = jnp.where(allowed, w_ref[g], 0.0).astype(BF16)
        cols = slice(g * SGU_BLOCK, (g + 1) * SGU_BLOCK)
        bias = bias_ref[:, cols]
        for blk in range(SGU_TM // SGU_BLOCK):
            rows = slice(blk * SGU_BLOCK, (blk + 1) * SGU_BLOCK)
            mix = _dot(wm, v_ref[rows, cols]) + bias
            o_ref[rows, cols] = (u_ref[rows, cols].astype(F32) * mix).astype(BF16)


def _sgu(mix, sgu_w, bias_full):
    m = mix.shape[0]
    return pl.pallas_call(
        _sgu_kernel,
        grid=(m // SGU_TM,),
        in_specs=[
            pl.BlockSpec((SGU_TM, SGU_WIDTH), lambda i: (i, COL_U // SGU_WIDTH)),
            pl.BlockSpec((SGU_TM, SGU_WIDTH), lambda i: (i, COL_V // SGU_WIDTH)),
            pl.BlockSpec((SGU_GROUPS, SGU_BLOCK, SGU_BLOCK), lambda i: (0, 0, 0)),
            pl.BlockSpec((SGU_BLOCK, SGU_WIDTH), lambda i: (0, 0)),
        ],
        out_specs=pl.BlockSpec((SGU_TM, SGU_WIDTH), lambda i: (i, 0)),
        out_shape=jax.ShapeDtypeStruct((m, SGU_WIDTH), BF16),
        compiler_params=_params(("parallel",)),
        name="sgu",
    )(mix, mix, sgu_w, bias_full)


ATT_TK = 512
ATT_TQ = 2 * ATT_TK
ATT_ROWS = 2 * ATT_TQ
AUG = 128


def _lane_tile(x, n):
    return jnp.concatenate([x] * n, axis=1)


def _attn_kernel(slopes_ref, lam_ref, q_ref, k_ref, v_ref, g_ref, o_ref,
                 kaug_scr, vaug_scr, diag_scr, qa_scr, sa_scr, sb_scr, m_scr, acc_scr):
    h = pl.program_id(1)
    qi = pl.program_id(2)
    tq, tk, rows = ATT_TQ, ATT_TK, ATT_ROWS
    seq = k_ref.shape[0]
    slope = slopes_ref[h]
    q0 = pl.multiple_of(qi * tq, tq)

    @pl.when(qi == 0)
    def _():
        pos = lax.broadcasted_iota(jnp.int32, (seq, AUG), 0)
        lane = lax.broadcasted_iota(jnp.int32, (seq, AUG), 1)
        feat = jnp.where(lane == 0, pos // CHUNK, jnp.where(lane == 1, pos % CHUNK, jnp.where(lane == 2, 1, 0)))
        kaug_scr[:, :V_HEAD_DIM] = k_ref[...]
        kaug_scr[:, V_HEAD_DIM:] = feat.astype(F32).astype(BF16)
        vaug_scr[:, :V_HEAD_DIM] = v_ref[...]
        vaug_scr[:, V_HEAD_DIM:] = jnp.ones((seq, AUG), BF16)
        t = lax.broadcasted_iota(jnp.int32, (tk, tk), 0)
        c = lax.broadcasted_iota(jnp.int32, (tk, tk), 1)
        ahead = jnp.maximum(c - t, 0).astype(F32)
        diag_scr[...] = jnp.where(c // CHUNK <= t // CHUNK, (-2.0 * slope) * ahead, MASK_VALUE)

    lane = lax.broadcasted_iota(jnp.int32, (tk, V_HEAD_DIM), 1)
    aug = jnp.where(lane == 0, CHUNK * slope,
                    jnp.where(lane == 1, slope, jnp.where(lane == 2, -slope * q0.astype(F32), 0.0))).astype(BF16)
    for half in range(2):
        q = q_ref[half * tk:(half + 1) * tk, :]
        zero = jnp.zeros_like(q)
        r1, r2 = 2 * half * tk, (2 * half + 1) * tk
        qa_scr[r1:r1 + tk, :V_HEAD_DIM] = jnp.where(lane < HEAD_DIM, q, zero)
        qa_scr[r2:r2 + tk, :V_HEAD_DIM] = jnp.where(lane >= HEAD_DIM, q, zero)
        qa_scr[r1:r1 + tk, V_HEAD_DIM:] = aug
        qa_scr[r2:r2 + tk, V_HEAD_DIM:] = aug
    m_scr[...] = jnp.full_like(m_scr, MASK_VALUE)
    acc_scr[...] = jnp.zeros_like(acc_scr)

    def scores(k0, r0=0):
        return lax.dot_general(qa_scr[r0:, :], kaug_scr[pl.ds(k0, tk), :], (((1,), (1,)), ((), ())),
                               preferred_element_type=F32)

    def update(s, k0, rs=slice(None)):
        width = s.shape[1]
        m_prev = m_scr[rs, :]
        m_new = jnp.maximum(m_prev, jnp.max(s, axis=1, keepdims=True))
        alpha = jnp.exp(m_prev - m_new)
        p = jnp.exp(s - _lane_tile(m_new, width // 128))
        m_scr[rs, :] = m_new
        acc_scr[rs, :] = (acc_scr[rs, :] * _lane_tile(alpha, 2)
                          + _dot(p.astype(BF16), vaug_scr[pl.ds(k0, width), :]))

    sa_scr[...] = scores(0)

    def body(pair, carry):
        k0 = pl.multiple_of(pair * (2 * tk), 2 * tk)
        sb_scr[...] = scores(k0 + tk)
        update(sa_scr[...], k0)
        sa_scr[...] = scores(k0 + 2 * tk)
        update(sb_scr[...], k0 + tk)
        return carry

    lax.fori_loop(0, qi, body, 0)

    first, second = slice(0, 2 * tk), slice(2 * tk, rows)
    diag2 = jnp.concatenate([diag_scr[...], diag_scr[...]], axis=0)
    sb_scr[second, :] = scores(q0 + tk, 2 * tk)
    update(sa_scr[first, :] + diag2, q0, first)
    update(jnp.concatenate([sa_scr[second, :], sb_scr[second, :] + diag2], axis=1), q0, second)

    lv = lam_ref[...]
    lam = (jnp.exp(jnp.sum(lv[0:1] * lv[1:2], axis=-1, keepdims=True))
           - jnp.exp(jnp.sum(lv[2:3] * lv[3:4], axis=-1, keepdims=True)) + LAM_INIT)
    for half in range(2):
        r1, r2 = 2 * half * tk, (2 * half + 1) * tk
        a1, a2 = acc_scr[r1:r1 + tk, :], acc_scr[r2:r2 + tk, :]
        o = a1[:, :V_HEAD_DIM] / a1[:, V_HEAD_DIM:] - lam * (a2[:, :V_HEAD_DIM] / a2[:, V_HEAD_DIM:])
        y = o * lax.rsqrt(jnp.mean(o * o, axis=-1, keepdims=True) + LN_EPS)
        o_ref[half * tk:(half + 1) * tk, :] = (y * g_ref[...] * (1.0 - LAM_INIT)).astype(BF16)


def _diff_attn(mix, slopes, lam_vecs, attn_g, batch, seq):
    m = mix.shape[0]
    nq = seq // ATT_TQ
    qc, kc, vc = COL_Q // V_HEAD_DIM, COL_K // V_HEAD_DIM, COL_VAL // V_HEAD_DIM
    return pl.pallas_call(
        _attn_kernel,
        grid=(batch, N_HEADS, nq),
        in_specs=[
            pl.BlockSpec(memory_space=pltpu.SMEM),
            pl.BlockSpec((4, HEAD_DIM), lambda b, h, i: (0, 0)),
            pl.BlockSpec((ATT_TQ, V_HEAD_DIM), lambda b, h, i: (b * nq + i, qc + h)),
            pl.BlockSpec((seq, V_HEAD_DIM), lambda b, h, i: (b, kc + h)),
            pl.BlockSpec((seq, V_HEAD_DIM), lambda b, h, i: (b, vc + h)),
            pl.BlockSpec((1, V_HEAD_DIM), lambda b, h, i: (0, h)),
        ],
        out_specs=pl.BlockSpec((ATT_TQ, V_HEAD_DIM), lambda b, h, i: (b * nq + i, h)),
        out_shape=jax.ShapeDtypeStruct((m, ATTN_WIDTH), BF16),
        scratch_shapes=[
            pltpu.VMEM((seq, V_HEAD_DIM + AUG), BF16),
            pltpu.VMEM((seq, V_HEAD_DIM + AUG), BF16),
            pltpu.VMEM((ATT_TK, ATT_TK), F32),
            pltpu.VMEM((ATT_ROWS, V_HEAD_DIM + AUG), BF16),
            pltpu.VMEM((ATT_ROWS, ATT_TK), F32),
            pltpu.VMEM((ATT_ROWS, ATT_TK), F32),
            pltpu.VMEM((ATT_ROWS, 128), F32),
            pltpu.VMEM((ATT_ROWS, V_HEAD_DIM + AUG), F32),
        ],
        compiler_params=_params(("parallel", "parallel", "arbitrary")),
        name="diff_attn",
    )(slopes, lam_vecs, mix, mix, mix, attn_g)


MERGE_TM = 512
MERGE_TC = 512


def _merge_kernel(xb_ref, xf_ref, ya_ref, yb_ref, wga_ref, wgb_ref, wa_hbm, wb_hbm, wo_hbm, g_ref, b_ref,
                  o_ref, acc_scr, pend_scr, wa_buf, wb_buf, wo_buf, wa_sem, wb_sem, wo_sem):
    i, j = pl.program_id(0), pl.program_id(1)
    nc = pl.num_programs(1)
    last = nc - 1
    step, total = _linear_step()

    def col0(s):
        return pl.multiple_of(lax.rem(s, nc) * MERGE_TC, MERGE_TC)

    slot = _stream_blocks([
        (lambda s: wa_hbm.at[:, pl.ds(col0(s), MERGE_TC)], wa_buf, wa_sem),
        (lambda s: wb_hbm.at[:, pl.ds(col0(s), MERGE_TC)], wb_buf, wb_sem),
        (lambda s: wo_hbm.at[pl.ds(col0(s), MERGE_TC), :], wo_buf, wo_sem),
    ], step, total)

    def partial():
        xb = xb_ref[...]
        merged = (jax.nn.sigmoid(_dot(xb, wga_ref[...])) * _dot(ya_ref[...], wa_buf[slot])
                  + jax.nn.sigmoid(_dot(xb, wgb_ref[...])) * _dot(yb_ref[...], wb_buf[slot]))
        return _dot(merged.astype(BF16), wo_buf[slot])

    def emit_norm():
        o_ref[...] = _layer_norm(pend_scr[...], g_ref[...], b_ref[...])

    @pl.when((i == 0) & (j == 0))
    def _():
        pend_scr[...] = jnp.zeros_like(pend_scr)

    @pl.when(j == 0)
    def _():
        acc_scr[...] = partial()
        emit_norm()

    @pl.when((j > 0) & (j < last))
    def _():
        acc_scr[...] += partial()

    @pl.when(j == last)
    def _():
        pend_scr[...] = ALPHA * xf_ref[...] + (acc_scr[...] + partial())

    @pl.when((j == last) & (i == pl.num_programs(0) - 1))
    def _():
        emit_norm()


def _merge_out(xb, xf, ya, yb, w_in, w_a, w_b, w_o, g, b):
    m, d = xf.shape
    nc = d // MERGE_TC
    ga0, gb0 = COL_GA // MERGE_TC, COL_GB // MERGE_TC
    row = lambda i, j: (i, 0)
    return pl.pallas_call(
        _merge_kernel,
        grid=(m // MERGE_TM, nc),
        in_specs=[
            pl.BlockSpec((MERGE_TM, d), row),
            pl.BlockSpec((MERGE_TM, d), row),
            pl.BlockSpec((MERGE_TM, SGU_WIDTH), row),
            pl.BlockSpec((MERGE_TM, ATTN_WIDTH), row),
            pl.BlockSpec((d, MERGE_TC), lambda i, j: (0, ga0 + j)),
            pl.BlockSpec((d, MERGE_TC), lambda i, j: (0, gb0 + j)),
            pl.BlockSpec(memory_space=pl.ANY),
            pl.BlockSpec(memory_space=pl.ANY),
            pl.BlockSpec(memory_space=pl.ANY),
            pl.BlockSpec((1, d), lambda i, j: (0, 0)),
            pl.BlockSpec((1, d), lambda i, j: (0, 0)),
        ],
        out_specs=pl.BlockSpec((MERGE_TM, d), _deferred_out_index),
        out_shape=jax.ShapeDtypeStruct((m, d), F32),
        scratch_shapes=[pltpu.VMEM((MERGE_TM, d), F32), pltpu.VMEM((MERGE_TM, d), F32),
                        pltpu.VMEM((2, SGU_WIDTH, MERGE_TC), BF16), pltpu.VMEM((2, ATTN_WIDTH, MERGE_TC), BF16),
                        pltpu.VMEM((2, MERGE_TC, d), BF16),
                        pltpu.SemaphoreType.DMA((2,)), pltpu.SemaphoreType.DMA((2,)),
                        pltpu.SemaphoreType.DMA((2,))],
        compiler_params=_params(("arbitrary", "arbitrary")),
        name="merge_out",
    )(xb, xf, ya, yb, w_in, w_in, w_a, w_b, w_o, g, b)


PE_TM = 512
PE_TN = 512


def _pe_kernel(xb_ref, xf_ref, p_ref, wg_hbm, wp_ref, g_ref, b_ref, o_ref, y_scr, pend_scr, wg_buf, wg_sem):
    i, j = pl.program_id(0), pl.program_id(1)
    nj = pl.num_programs(1)
    last = nj - 1
    c_last = y_scr.shape[1] - PE_TN
    step, total = _linear_step()

    def wg_block(s):
        return wg_hbm.at[:, pl.ds(pl.multiple_of(lax.rem(s, nj) * PE_TN, PE_TN), PE_TN)]

    slot = _stream_blocks([(wg_block, wg_buf, wg_sem)], step, total)

    def chunk(c0):
        gate = jax.nn.sigmoid(_dot(xb_ref[...], wg_buf[slot]))
        emb = _dot(p_ref[...].astype(BF16), wp_ref[...])
        return ALPHA * xf_ref[:, pl.ds(c0, PE_TN)] + gate * emb

    def emit_norm():
        o_ref[...] = _layer_norm(pend_scr[...], g_ref[...], b_ref[...])

    @pl.when((i == 0) & (j == 0))
    def _():
        pend_scr[...] = jnp.zeros_like(pend_scr)

    @pl.when(j == 0)
    def _():
        y_scr[:, :PE_TN] = chunk(0)
        emit_norm()

    @pl.when((j > 0) & (j < last))
    def _():
        c0 = pl.multiple_of(j * PE_TN, PE_TN)
        y_scr[:, pl.ds(c0, PE_TN)] = chunk(c0)

    @pl.when(j == last)
    def _():
        pend_scr[:, c_last:] = chunk(c_last)
        pend_scr[:, :c_last] = y_scr[:, :c_last]

    @pl.when((j == last) & (i == pl.num_programs(0) - 1))
    def _():
        emit_norm()


def _pe_gate(xb, xf, p, w_g, w_p, g, b):
    m, d = xf.shape
    row = lambda i, j: (i, 0)
    return pl.pallas_call(
        _pe_kernel,
        grid=(m // PE_TM, d // PE_TN),
        in_specs=[
            pl.BlockSpec((PE_TM, d), row),
            pl.BlockSpec((PE_TM, d), row),
            pl.BlockSpec((PE_TM, P_DIM), row),
            pl.BlockSpec(memory_space=pl.ANY),
            pl.BlockSpec((P_DIM, PE_TN), lambda i, j: (0, j)),
            pl.BlockSpec((1, d), lambda i, j: (0, 0)),
            pl.BlockSpec((1, d), lambda i, j: (0, 0)),
        ],
        out_specs=pl.BlockSpec((PE_TM, d), _deferred_out_index),
        out_shape=jax.ShapeDtypeStruct((m, d), F32),
        scratch_shapes=[pltpu.VMEM((PE_TM, d), F32), pltpu.VMEM((PE_TM, d), F32),
                        pltpu.VMEM((2, d, PE_TN), BF16), pltpu.SemaphoreType.DMA((2,))],
        compiler_params=_params(("arbitrary", "arbitrary")),
        name="pe_gate",
    )(xb, xf, p, w_g, w_p, g, b)


def kernel(x, p, ffn1_w_gu, ffn1_w_down, ln1_g, ln1_b, w_in, sgu_ln_g, sgu_ln_b, sgu_w, sgu_b, lam_q1, lam_k1, lam_q2, lam_k2, attn_norm_g, w_branch_a, w_branch_b, w_out, ln2_g, ln2_b, ffn2_w_gu, ffn2_w_down, ln3_g, ln3_b, w_pe_gate, w_pe_proj, ln4_g, ln4_b):
    batch, seq, d = x.shape
    m = batch * seq
    slopes = jnp.asarray(2.0 ** (-8.0 * np.arange(1, N_HEADS + 1) / N_HEADS), dtype=F32)
    xf = x.reshape(m, d)
    for i in range(DEPTH):
        w_in_b = w_in[i].astype(BF16)
        bias_full = jnp.repeat(sgu_b[i].T, SGU_BLOCK, axis=1)
        lam_vecs = jnp.stack([lam_q1[i], lam_k1[i], lam_q2[i], lam_k2[i]]).astype(F32)

        x1f, x1b = _ffn_ln(xf, ffn1_w_gu[i].astype(BF16), ffn1_w_down[i].astype(BF16),
                           ln1_g[i][None], ln1_b[i][None])
        mix = _in_proj(x1b, w_in_b, sgu_ln_g[i][None], sgu_ln_b[i][None])
        y_a = _sgu(mix, sgu_w[i], bias_full)
        y_b = _diff_attn(mix, slopes, lam_vecs, attn_norm_g[i][None], batch, seq)
        x2f = _merge_out(x1b, x1f, y_a, y_b, w_in_b, w_branch_a[i].astype(BF16), w_branch_b[i].astype(BF16),
                         w_out[i].astype(BF16), ln2_g[i][None], ln2_b[i][None])
        x3f, x3b = _ffn_ln(x2f, ffn2_w_gu[i].astype(BF16), ffn2_w_down[i].astype(BF16),
                           ln3_g[i][None], ln3_b[i][None])
        xf = _pe_gate(x3b, x3f, p[i].reshape(m, P_DIM), w_pe_gate[i].astype(BF16), w_pe_proj[i].astype(BF16),
                      ln4_g[i][None], ln4_b[i][None])
    return xf.reshape(batch, seq, d)
```

```python
import math

import jax
import jax.numpy as jnp
import numpy as np
from jax import lax
from jax.experimental import pallas as pl
from jax.experimental.pallas import tpu as pltpu

CHUNK = 64
P_DIM = 256
SGU_BLOCK = 128
SGU_GROUPS = 8
SGU_WIDTH = 1024
N_HEADS = 8
HEAD_DIM = 64
V_HEAD_DIM = 128
ATTN_WIDTH = 1024
DEPTH = 1
ALPHA = (2 * DEPTH) ** 0.25
LN_EPS = 1e-5
LAM_INIT = 0.8 - 0.6 * math.exp(-0.3 * 0)

COL_U, COL_V, COL_Q, COL_K, COL_VAL = 0, 1024, 2048, 3072, 4096
COL_GA, COL_GB = 5120, 7168
MIX_COLS = 5120

BF16 = jnp.bfloat16
F32 = jnp.float32

VMEM_LIMIT_BYTES = 56 * 1024 * 1024

MASK_VALUE = -1e30


def _params(semantics):
    return pltpu.CompilerParams(dimension_semantics=semantics, vmem_limit_bytes=VMEM_LIMIT_BYTES)


def _dot(a, b):
    return jnp.dot(a, b, preferred_element_type=F32)


def _layer_norm(y, g, b):
    mu = jnp.mean(y, axis=-1, keepdims=True)
    d = y - mu
    var = jnp.mean(d * d, axis=-1, keepdims=True)
    return d * lax.rsqrt(var + LN_EPS) * g + b


FFN_TM = 512
FFN_TF = 512


def _ffn_ln_kernel(x_ref, wg_ref, wu_ref, wd_ref, g_ref, b_ref, of_ref, ob_ref, xb_scr, acc_scr):
    j = pl.program_id(1)

    @pl.when(j == 0)
    def _():
        xb_scr[...] = x_ref[...].astype(BF16)
        acc_scr[...] = jnp.zeros_like(acc_scr)

    xb = xb_scr[...]
    gate = _dot(xb, wg_ref[...])
    up = _dot(xb, wu_ref[...])
    h = (gate * jax.nn.sigmoid(gate)) * up
    acc_scr[...] += _dot(h.astype(BF16), wd_ref[...])

    @pl.when(j == pl.num_programs(1) - 1)
    def _():
        y = ALPHA * x_ref[...] + 0.5 * acc_scr[...]
        out = _layer_norm(y, g_ref[...], b_ref[...])
        of_ref[...] = out
        ob_ref[...] = out.astype(BF16)


def _ffn_ln(x, w_gu, w_down, g, b):
    m, d = x.shape
    f = w_down.shape[0]
    nf = f // FFN_TF
    return pl.pallas_call(
        _ffn_ln_kernel,
        grid=(m // FFN_TM, nf),
        in_specs=[
            pl.BlockSpec((FFN_TM, d), lambda i, j: (i, 0)),
            pl.BlockSpec((d, FFN_TF), lambda i, j: (0, j)),
            pl.BlockSpec((d, FFN_TF), lambda i, j: (0, nf + j)),
            pl.BlockSpec((FFN_TF, d), lambda i, j: (j, 0)),
            pl.BlockSpec((1, d), lambda i, j: (0, 0)),
            pl.BlockSpec((1, d), lambda i, j: (0, 0)),
        ],
        out_specs=[
            pl.BlockSpec((FFN_TM, d), lambda i, j: (i, 0)),
            pl.BlockSpec((FFN_TM, d), lambda i, j: (i, 0)),
        ],
        out_shape=[jax.ShapeDtypeStruct((m, d), F32), jax.ShapeDtypeStruct((m, d), BF16)],
        scratch_shapes=[pltpu.VMEM((FFN_TM, d), BF16), pltpu.VMEM((FFN_TM, d), F32)],
        compiler_params=_params(("parallel", "arbitrary")),
        name="ffn_ln",
    )(x, w_gu, w_gu, w_down, g, b)


PROJ_TM = 1024
PROJ_TN = 1024


def _in_proj_kernel(x_ref, w_ref, g_ref, b_ref, o_ref):
    j = pl.program_id(1)

    def proj():
        return _dot(x_ref[...], w_ref[...])

    @pl.when(j == COL_U // PROJ_TN)
    def _():
        o_ref[...] = jax.nn.gelu(proj()).astype(BF16)

    @pl.when(j == COL_V // PROJ_TN)
    def _():
        o_ref[...] = _layer_norm(jax.nn.gelu(proj()), g_ref[...], b_ref[...]).astype(BF16)

    @pl.when(j == COL_Q // PROJ_TN)
    def _():
        o_ref[...] = (proj() * HEAD_DIM ** -0.5).astype(BF16)

    @pl.when(j >= COL_K // PROJ_TN)
    def _():
        o_ref[...] = proj().astype(BF16)


def _in_proj(xb, w_in, sgu_g, sgu_b):
    m, d = xb.shape
    return pl.pallas_call(
        _in_proj_kernel,
        grid=(m // PROJ_TM, MIX_COLS // PROJ_TN),
        in_specs=[
            pl.BlockSpec((PROJ_TM, d), lambda i, j: (i, 0)),
            pl.BlockSpec((d, PROJ_TN), lambda i, j: (0, j)),
            pl.BlockSpec((1, SGU_WIDTH), lambda i, j: (0, 0)),
            pl.BlockSpec((1, SGU_WIDTH), lambda i, j: (0, 0)),
        ],
        out_specs=pl.BlockSpec((PROJ_TM, PROJ_TN), lambda i, j: (i, j)),
        out_shape=jax.ShapeDtypeStruct((m, MIX_COLS), BF16),
        compiler_params=_params(("parallel", "arbitrary")),
        name="in_proj",
    )(xb, w_in, sgu_g, sgu_b)


SGU_TM = 512


def _sgu_kernel(u_ref, v_ref, w_ref, bias_ref, o_ref):
    t = lax.broadcasted_iota(jnp.int32, (SGU_BLOCK, SGU_BLOCK), 0)
    s = lax.broadcasted_iota(jnp.int32, (SGU_BLOCK, SGU_BLOCK), 1)
    allowed = (s // CHUNK) <= (t // CHUNK)
    for g in range(SGU_GROUPS):
        wm = jnp.where(allowed, w_ref[g], 0.0).astype(BF16)
        cols = slice(g * SGU_BLOCK, (g + 1) * SGU_BLOCK)
        bias = bias_ref[:, cols]
        for blk in range(SGU_TM // SGU_BLOCK):
            rows = slice(blk * SGU_BLOCK, (blk + 1) * SGU_BLOCK)
            mix = _dot(wm, v_ref[rows, cols]) + bias
            o_ref[rows, cols] = (u_ref[rows, cols].astype(F32) * mix).astype(BF16)


def _sgu(mix, sgu_w, bias_full):
    m = mix.shape[0]
    return pl.pallas_call(
        _sgu_kernel,
        grid=(m // SGU_TM,),
        in_specs=[
            pl.BlockSpec((SGU_TM, SGU_WIDTH), lambda i: (i, COL_U // SGU_WIDTH)),
            pl.BlockSpec((SGU_TM, SGU_WIDTH), lambda i: (i, COL_V // SGU_WIDTH)),
            pl.BlockSpec((SGU_GROUPS, SGU_BLOCK, SGU_BLOCK), lambda i: (0, 0, 0)),
            pl.BlockSpec((SGU_BLOCK, SGU_WIDTH), lambda i: (0, 0)),
        ],
        out_specs=pl.BlockSpec((SGU_TM, SGU_WIDTH), lambda i: (i, 0)),
        out_shape=jax.ShapeDtypeStruct((m, SGU_WIDTH), BF16),
        compiler_params=_params(("parallel",)),
        name="sgu",
    )(mix, mix, sgu_w, bias_full)


ATT_TK = 512
ATT_TQ = 2 * ATT_TK
ATT_ROWS = 2 * ATT_TQ
AUG = 128


def _lane_tile(x, n):
    return jnp.concatenate([x] * n, axis=1)


def _attn_kernel(slopes_ref, lam_ref, q_ref, k_ref, v_ref, g_ref, o_ref,
                 kaug_scr, vaug_scr, diag_scr, qa_scr, sa_scr, sb_scr, m_scr, acc_scr):
    h = pl.program_id(1)
    qi = pl.program_id(2)
    tq, tk, rows = ATT_TQ, ATT_TK, ATT_ROWS
    seq = k_ref.shape[0]
    slope = slopes_ref[h]
    q0 = pl.multiple_of(qi * tq, tq)

    @pl.when(qi == 0)
    def _():
        pos = lax.broadcasted_iota(jnp.int32, (seq, AUG), 0)
        lane = lax.broadcasted_iota(jnp.int32, (seq, AUG), 1)
        feat = jnp.where(lane == 0, pos // CHUNK, jnp.where(lane == 1, pos % CHUNK, jnp.where(lane == 2, 1, 0)))
        kaug_scr[:, :V_HEAD_DIM] = k_ref[...]
        kaug_scr[:, V_HEAD_DIM:] = feat.astype(F32).astype(BF16)
        vaug_scr[:, :V_HEAD_DIM] = v_ref[...]
        vaug_scr[:, V_HEAD_DIM:] = jnp.ones((seq, AUG), BF16)
        t = lax.broadcasted_iota(jnp.int32, (tk, tk), 0)
        c = lax.broadcasted_iota(jnp.int32, (tk, tk), 1)
        ahead = jnp.maximum(c - t, 0).astype(F32)
        diag_scr[...] = jnp.where(c // CHUNK <= t // CHUNK, (-2.0 * slope) * ahead, MASK_VALUE)

    lane = lax.broadcasted_iota(jnp.int32, (tk, V_HEAD_DIM), 1)
    aug = jnp.where(lane == 0, CHUNK * slope,
                    jnp.where(lane == 1, slope, jnp.where(lane == 2, -slope * q0.astype(F32), 0.0))).astype(BF16)
    for half in range(2):
        q = q_ref[half * tk:(half + 1) * tk, :]
        zero = jnp.zeros_like(q)
        r1, r2 = 2 * half * tk, (2 * half + 1) * tk
        qa_scr[r1:r1 + tk, :V_HEAD_DIM] = jnp.where(lane < HEAD_DIM, q, zero)
        qa_scr[r2:r2 + tk, :V_HEAD_DIM] = jnp.where(lane >= HEAD_DIM, q, zero)
        qa_scr[r1:r1 + tk, V_HEAD_DIM:] = aug
        qa_scr[r2:r2 + tk, V_HEAD_DIM:] = aug
    m_scr[...] = jnp.full_like(m_scr, MASK_VALUE)
    acc_scr[...] = jnp.zeros_like(acc_scr)

    def scores(k0, r0=0):
        return lax.dot_general(qa_scr[r0:, :], kaug_scr[pl.ds(k0, tk), :], (((1,), (1,)), ((), ())),
                               preferred_element_type=F32)

    def update(s, k0, rs=slice(None)):
        width = s.shape[1]
        m_prev = m_scr[rs, :]
        m_new = jnp.maximum(m_prev, jnp.max(s, axis=1, keepdims=True))
        alpha = jnp.exp(m_prev - m_new)
        p = jnp.exp(s - _lane_tile(m_new, width // 128))
        m_scr[rs, :] = m_new
        acc_scr[rs, :] = (acc_scr[rs, :] * _lane_tile(alpha, 2)
                          + _dot(p.astype(BF16), vaug_scr[pl.ds(k0, width), :]))

    sa_scr[...] = scores(0)

    def body(pair, carry):
        k0 = pl.multiple_of(pair * (2 * tk), 2 * tk)
        sb_scr[...] = scores(k0 + tk)
        update(sa_scr[...], k0)
        sa_scr[...] = scores(k0 + 2 * tk)
        update(sb_scr[...], k0 + tk)
        return carry

    lax.fori_loop(0, qi, body, 0)

    first, second = slice(0, 2 * tk), slice(2 * tk, rows)
    diag2 = jnp.concatenate([diag_scr[...], diag_scr[...]], axis=0)
    sb_scr[second, :] = scores(q0 + tk, 2 * tk)
    update(sa_scr[first, :] + diag2, q0, first)
    update(jnp.concatenate([sa_scr[second, :], sb_scr[second, :] + diag2], axis=1), q0, second)

    lv = lam_ref[...]
    lam = (jnp.exp(jnp.sum(lv[0:1] * lv[1:2], axis=-1, keepdims=True))
           - jnp.exp(jnp.sum(lv[2:3] * lv[3:4], axis=-1, keepdims=True)) + LAM_INIT)
    for half in range(2):
        r1, r2 = 2 * half * tk, (2 * half + 1) * tk
        a1, a2 = acc_scr[r1:r1 + tk, :], acc_scr[r2:r2 + tk, :]
        o = a1[:, :V_HEAD_DIM] / a1[:, V_HEAD_DIM:] - lam * (a2[:, :V_HEAD_DIM] / a2[:, V_HEAD_DIM:])
        y = o * lax.rsqrt(jnp.mean(o * o, axis=-1, keepdims=True) + LN_EPS)
        o_ref[half * tk:(half + 1) * tk, :] = (y * g_ref[...] * (1.0 - LAM_INIT)).astype(BF16)


def _diff_attn(mix, slopes, lam_vecs, attn_g, batch, seq):
    m = mix.shape[0]
    nq = seq // ATT_TQ
    qc, kc, vc = COL_Q // V_HEAD_DIM, COL_K // V_HEAD_DIM, COL_VAL // V_HEAD_DIM
    return pl.pallas_call(
        _attn_kernel,
        grid=(batch, N_HEADS, nq),
        in_specs=[
            pl.BlockSpec(memory_space=pltpu.SMEM),
            pl.BlockSpec((4, HEAD_DIM), lambda b, h, i: (0, 0)),
            pl.BlockSpec((ATT_TQ, V_HEAD_DIM), lambda b, h, i: (b * nq + i, qc + h)),
            pl.BlockSpec((seq, V_HEAD_DIM), lambda b, h, i: (b, kc + h)),
            pl.BlockSpec((seq, V_HEAD_DIM), lambda b, h, i: (b, vc + h)),
            pl.BlockSpec((1, V_HEAD_DIM), lambda b, h, i: (0, h)),
        ],
        out_specs=pl.BlockSpec((ATT_TQ, V_HEAD_DIM), lambda b, h, i: (b * nq + i, h)),
        out_shape=jax.ShapeDtypeStruct((m, ATTN_WIDTH), BF16),
        scratch_shapes=[
            pltpu.VMEM((seq, V_HEAD_DIM + AUG), BF16),
            pltpu.VMEM((seq, V_HEAD_DIM + AUG), BF16),
            pltpu.VMEM((ATT_TK, ATT_TK), F32),
            pltpu.VMEM((ATT_ROWS, V_HEAD_DIM + AUG), BF16),
            pltpu.VMEM((ATT_ROWS, ATT_TK), F32),
            pltpu.VMEM((ATT_ROWS, ATT_TK), F32),
            pltpu.VMEM((ATT_ROWS, 128), F32),
            pltpu.VMEM((ATT_ROWS, V_HEAD_DIM + AUG), F32),
        ],
        compiler_params=_params(("parallel", "parallel", "arbitrary")),
        name="diff_attn",
    )(slopes, lam_vecs, mix, mix, mix, attn_g)


MERGE_TM = 512
MERGE_TC = 512


def _merge_kernel(xb_ref, xf_ref, ya_ref, yb_ref, wga_ref, wgb_ref, wa_ref, wb_ref, wo_ref, g_ref, b_ref,
                  o_ref, acc_scr):
    j = pl.program_id(1)

    @pl.when(j == 0)
    def _():
        acc_scr[...] = jnp.zeros_like(acc_scr)

    xb = xb_ref[...]
    merged = (jax.nn.sigmoid(_dot(xb, wga_ref[...])) * _dot(ya_ref[...], wa_ref[...])
              + jax.nn.sigmoid(_dot(xb, wgb_ref[...])) * _dot(yb_ref[...], wb_ref[...]))
    acc_scr[...] += _dot(merged.astype(BF16), wo_ref[...])

    @pl.when(j == pl.num_programs(1) - 1)
    def _():
        o_ref[...] = _layer_norm(ALPHA * xf_ref[...] + acc_scr[...], g_ref[...], b_ref[...])


def _merge_out(xb, xf, ya, yb, w_in, w_a, w_b, w_o, g, b):
    m, d = xf.shape
    nc = d // MERGE_TC
    ga0, gb0 = COL_GA // MERGE_TC, COL_GB // MERGE_TC
    row = lambda i, j: (i, 0)
    return pl.pallas_call(
        _merge_kernel,
        grid=(m // MERGE_TM, nc),
        in_specs=[
            pl.BlockSpec((MERGE_TM, d), row),
            pl.BlockSpec((MERGE_TM, d), row),
            pl.BlockSpec((MERGE_TM, SGU_WIDTH), row),
            pl.BlockSpec((MERGE_TM, ATTN_WIDTH), row),
            pl.BlockSpec((d, MERGE_TC), lambda i, j: (0, ga0 + j)),
            pl.BlockSpec((d, MERGE_TC), lambda i, j: (0, gb0 + j)),
            pl.BlockSpec((SGU_WIDTH, MERGE_TC), lambda i, j: (0, j)),
            pl.BlockSpec((ATTN_WIDTH, MERGE_TC), lambda i, j: (0, j)),
            pl.BlockSpec((MERGE_TC, d), lambda i, j: (j, 0)),
            pl.BlockSpec((1, d), lambda i, j: (0, 0)),
            pl.BlockSpec((1, d), lambda i, j: (0, 0)),
        ],
        out_specs=pl.BlockSpec((MERGE_TM, d), row),
        out_shape=jax.ShapeDtypeStruct((m, d), F32),
        scratch_shapes=[pltpu.VMEM((MERGE_TM, d), F32)],
        compiler_params=_params(("parallel", "arbitrary")),
        name="merge_out",
    )(xb, xf, ya, yb, w_in, w_in, w_a, w_b, w_o, g, b)


PE_TM = 512
PE_SPLIT = 2


def _pe_kernel(xb_ref, xf_ref, p_ref, wg_ref, wp_ref, g_ref, b_ref, o_ref):
    sub = PE_TM // PE_SPLIT
    for part in range(PE_SPLIT):
        rows = slice(part * sub, (part + 1) * sub)
        gate = jax.nn.sigmoid(_dot(xb_ref[rows, :], wg_ref[...]))
        emb = _dot(p_ref[rows, :].astype(BF16), wp_ref[...])
        o_ref[rows, :] = _layer_norm(ALPHA * xf_ref[rows, :] + gate * emb, g_ref[...], b_ref[...])


def _pe_gate(xb, xf, p, w_g, w_p, g, b):
    m, d = xf.shape
    row = lambda i: (i, 0)
    whole = lambda i: (0, 0)
    return pl.pallas_call(
        _pe_kernel,
        grid=(m // PE_TM,),
        in_specs=[
            pl.BlockSpec((PE_TM, d), row),
            pl.BlockSpec((PE_TM, d), row),
            pl.BlockSpec((PE_TM, P_DIM), row),
            pl.BlockSpec((d, d), whole),
            pl.BlockSpec((P_DIM, d), whole),
            pl.BlockSpec((1, d), whole),
            pl.BlockSpec((1, d), whole),
        ],
        out_specs=pl.BlockSpec((PE_TM, d), row),
        out_shape=jax.ShapeDtypeStruct((m, d), F32),
        compiler_params=_params(("parallel",)),
        name="pe_gate",
    )(xb, xf, p, w_g, w_p, g, b)


def kernel(x, p, ffn1_w_gu, ffn1_w_down, ln1_g, ln1_b, w_in, sgu_ln_g, sgu_ln_b, sgu_w, sgu_b, lam_q1, lam_k1, lam_q2, lam_k2, attn_norm_g, w_branch_a, w_branch_b, w_out, ln2_g, ln2_b, ffn2_w_gu, ffn2_w_down, ln3_g, ln3_b, w_pe_gate, w_pe_proj, ln4_g, ln4_b):
    batch, seq, d = x.shape
    m = batch * seq
    slopes = jnp.asarray(2.0 ** (-8.0 * np.arange(1, N_HEADS + 1) / N_HEADS), dtype=F32)
    xf = x.reshape(m, d)
    for i in range(DEPTH):
        w_in_b = w_in[i].astype(BF16)
        bias_full = jnp.repeat(sgu_b[i].T, SGU_BLOCK, axis=1)
        lam_vecs = jnp.stack([lam_q1[i], lam_k1[i], lam_q2[i], lam_k2[i]]).astype(F32)

        x1f, x1b = _ffn_ln(xf, ffn1_w_gu[i].astype(BF16), ffn1_w_down[i].astype(BF16),
                           ln1_g[i][None], ln1_b[i][None])
        mix = _in_proj(x1b, w_in_b, sgu_ln_g[i][None], sgu_ln_b[i][None])
        y_a = _sgu(mix, sgu_w[i], bias_full)
        y_b = _diff_attn(mix, slopes, lam_vecs, attn_norm_g[i][None], batch, seq)
        x2f = _merge_out(x1b, x1f, y_a, y_b, w_in_b, w_branch_a[i].astype(BF16), w_branch_b[i].astype(BF16),
                         w_out[i].astype(BF16), ln2_g[i][None], ln2_b[i][None])
        x3f, x3b = _ffn_ln(x2f, ffn2_w_gu[i].astype(BF16), ffn2_w_down[i].astype(BF16),
                           ln3_g[i][None], ln3_b[i][None])
        xf = _pe_gate(x3b, x3f, p[i].reshape(m, P_DIM), w_pe_gate[i].astype(BF16), w_pe_proj[i].astype(BF16),
                      ln4_g[i][None], ln4_b[i][None])
    return xf.reshape(batch, seq, d)
```

```python
import math

import jax
import jax.numpy as jnp
import numpy as np
from jax import lax
from jax.experimental import pallas as pl
from jax.experimental.pallas import tpu as pltpu

CHUNK = 64
P_DIM = 256
SGU_BLOCK = 128
SGU_GROUPS = 8
SGU_WIDTH = 1024
N_HEADS = 8
HEAD_DIM = 64
V_HEAD_DIM = 128
ATTN_WIDTH = 1024
DEPTH = 1
ALPHA = (2 * DEPTH) ** 0.25
LN_EPS = 1e-5
LAM_INIT = 0.8 - 0.6 * math.exp(-0.3 * 0)

COL_U, COL_V, COL_Q, COL_K, COL_VAL = 0, 1024, 2048, 3072, 4096
COL_GA, COL_GB = 5120, 7168
MIX_COLS = 5120

BF16 = jnp.bfloat16
F32 = jnp.float32

VMEM_LIMIT_BYTES = 56 * 1024 * 1024

MASK_VALUE = -1e30


def _params(semantics):
    return pltpu.CompilerParams(dimension_semantics=semantics, vmem_limit_bytes=VMEM_LIMIT_BYTES)


def _dot(a, b):
    return jnp.dot(a, b, preferred_element_type=F32)


def _layer_norm(y, g, b):
    mu = jnp.mean(y, axis=-1, keepdims=True)
    d = y - mu
    var = jnp.mean(d * d, axis=-1, keepdims=True)
    return d * lax.rsqrt(var + LN_EPS) * g + b


FFN_TM = 512
FFN_TF = 512
FFN_EDGE_SPLIT = 2


def _row_parts(n_rows, parts):
    size = n_rows // parts
    return [slice(k * size, (k + 1) * size) for k in range(parts)]


def _ffn_ln_kernel(x_ref, wg_ref, wu_ref, wd_ref, g_ref, b_ref, of_ref, ob_ref, xb_scr, acc_scr):
    j = pl.program_id(1)
    last = pl.num_programs(1) - 1

    def partial(rows):
        xb = xb_scr[rows, :]
        gate = _dot(xb, wg_ref[...])
        up = _dot(xb, wu_ref[...])
        h = (gate * jax.nn.sigmoid(gate)) * up
        return _dot(h.astype(BF16), wd_ref[...])

    @pl.when(j == 0)
    def _():
        for rows in _row_parts(FFN_TM, FFN_EDGE_SPLIT):
            xb_scr[rows, :] = x_ref[rows, :].astype(BF16)
            acc_scr[rows, :] = partial(rows)

    @pl.when((j > 0) & (j < last))
    def _():
        acc_scr[...] += partial(slice(None))

    @pl.when(j == last)
    def _():
        for rows in _row_parts(FFN_TM, FFN_EDGE_SPLIT):
            y = ALPHA * x_ref[rows, :] + 0.5 * (acc_scr[rows, :] + partial(rows))
            out = _layer_norm(y, g_ref[...], b_ref[...])
            of_ref[rows, :] = out
            ob_ref[rows, :] = out.astype(BF16)


def _ffn_ln(x, w_gu, w_down, g, b):
    m, d = x.shape
    f = w_down.shape[0]
    nf = f // FFN_TF
    return pl.pallas_call(
        _ffn_ln_kernel,
        grid=(m // FFN_TM, nf),
        in_specs=[
            pl.BlockSpec((FFN_TM, d), lambda i, j: (i, 0)),
            pl.BlockSpec((d, FFN_TF), lambda i, j: (0, j)),
            pl.BlockSpec((d, FFN_TF), lambda i, j: (0, nf + j)),
            pl.BlockSpec((FFN_TF, d), lambda i, j: (j, 0)),
            pl.BlockSpec((1, d), lambda i, j: (0, 0)),
            pl.BlockSpec((1, d), lambda i, j: (0, 0)),
        ],
        out_specs=[
            pl.BlockSpec((FFN_TM, d), lambda i, j: (i, 0)),
            pl.BlockSpec((FFN_TM, d), lambda i, j: (i, 0)),
        ],
        out_shape=[jax.ShapeDtypeStruct((m, d), F32), jax.ShapeDtypeStruct((m, d), BF16)],
        scratch_shapes=[pltpu.VMEM((FFN_TM, d), BF16), pltpu.VMEM((FFN_TM, d), F32)],
        compiler_params=_params(("parallel", "arbitrary")),
        name="ffn_ln",
    )(x, w_gu, w_gu, w_down, g, b)


PROJ_TM = 1024
PROJ_TN = 1024


def _in_proj_kernel(x_ref, w_ref, g_ref, b_ref, o_ref):
    j = pl.program_id(1)

    def proj():
        return _dot(x_ref[...], w_ref[...])

    @pl.when(j == COL_U // PROJ_TN)
    def _():
        o_ref[...] = jax.nn.gelu(proj()).astype(BF16)

    @pl.when(j == COL_V // PROJ_TN)
    def _():
        o_ref[...] = _layer_norm(jax.nn.gelu(proj()), g_ref[...], b_ref[...]).astype(BF16)

    @pl.when(j == COL_Q // PROJ_TN)
    def _():
        o_ref[...] = (proj() * HEAD_DIM ** -0.5).astype(BF16)

    @pl.when(j >= COL_K // PROJ_TN)
    def _():
        o_ref[...] = proj().astype(BF16)


def _in_proj(xb, w_in, sgu_g, sgu_b):
    m, d = xb.shape
    return pl.pallas_call(
        _in_proj_kernel,
        grid=(m // PROJ_TM, MIX_COLS // PROJ_TN),
        in_specs=[
            pl.BlockSpec((PROJ_TM, d), lambda i, j: (i, 0)),
            pl.BlockSpec((d, PROJ_TN), lambda i, j: (0, j)),
            pl.BlockSpec((1, SGU_WIDTH), lambda i, j: (0, 0)),
            pl.BlockSpec((1, SGU_WIDTH), lambda i, j: (0, 0)),
        ],
        out_specs=pl.BlockSpec((PROJ_TM, PROJ_TN), lambda i, j: (i, j)),
        out_shape=jax.ShapeDtypeStruct((m, MIX_COLS), BF16),
        compiler_params=_params(("parallel", "arbitrary")),
        name="in_proj",
    )(xb, w_in, sgu_g, sgu_b)


SGU_TM = 512


def _sgu_kernel(u_ref, v_ref, w_ref, bias_ref, o_ref):
    t = lax.broadcasted_iota(jnp.int32, (SGU_BLOCK, SGU_BLOCK), 0)
    s = lax.broadcasted_iota(jnp.int32, (SGU_BLOCK, SGU_BLOCK), 1)
    allowed = (s // CHUNK) <= (t // CHUNK)
    for g in range(SGU_GROUPS):
        wm = jnp.where(allowed, w_ref[g], 0.0).astype(BF16)
        cols = slice(g * SGU_BLOCK, (g + 1) * SGU_BLOCK)
        bias = bias_ref[:, cols]
        for blk in range(SGU_TM // SGU_BLOCK):
            rows = slice(blk * SGU_BLOCK, (blk + 1) * SGU_BLOCK)
            mix = _dot(wm, v_ref[rows, cols]) + bias
            o_ref[rows, cols] = (u_ref[rows, cols].astype(F32) * mix).astype(BF16)


def _sgu(mix, sgu_w, bias_full):
    m = mix.shape[0]
    return pl.pallas_call(
        _sgu_kernel,
        grid=(m // SGU_TM,),
        in_specs=[
            pl.BlockSpec((SGU_TM, SGU_WIDTH), lambda i: (i, COL_U // SGU_WIDTH)),
            pl.BlockSpec((SGU_TM, SGU_WIDTH), lambda i: (i, COL_V // SGU_WIDTH)),
            pl.BlockSpec((SGU_GROUPS, SGU_BLOCK, SGU_BLOCK), lambda i: (0, 0, 0)),
            pl.BlockSpec((SGU_BLOCK, SGU_WIDTH), lambda i: (0, 0)),
        ],
        out_specs=pl.BlockSpec((SGU_TM, SGU_WIDTH), lambda i: (i, 0)),
        out_shape=jax.ShapeDtypeStruct((m, SGU_WIDTH), BF16),
        compiler_params=_params(("parallel",)),
        name="sgu",
    )(mix, mix, sgu_w, bias_full)


ATT_TK = 512
ATT_TQ = 2 * ATT_TK
ATT_ROWS = 2 * ATT_TQ
AUG = 128


def _lane_tile(x, n):
    return jnp.concatenate([x] * n, axis=1)


def _attn_kernel(slopes_ref, lam_ref, q_ref, k_ref, v_ref, g_ref, o_ref,
                 kaug_scr, vaug_scr, diag_scr, qa_scr, sa_scr, sb_scr, m_scr, acc_scr):
    h = pl.program_id(1)
    qi = pl.program_id(2)
    tq, tk, rows = ATT_TQ, ATT_TK, ATT_ROWS
    seq = k_ref.shape[0]
    slope = slopes_ref[h]
    q0 = pl.multiple_of(qi * tq, tq)

    @pl.when(qi == 0)
    def _():
        pos = lax.broadcasted_iota(jnp.int32, (seq, AUG), 0)
        lane = lax.broadcasted_iota(jnp.int32, (seq, AUG), 1)
        feat = jnp.where(lane == 0, pos // CHUNK, jnp.where(lane == 1, pos % CHUNK, jnp.where(lane == 2, 1, 0)))
        kaug_scr[:, :V_HEAD_DIM] = k_ref[...]
        kaug_scr[:, V_HEAD_DIM:] = feat.astype(F32).astype(BF16)
        vaug_scr[:, :V_HEAD_DIM] = v_ref[...]
        vaug_scr[:, V_HEAD_DIM:] = jnp.ones((seq, AUG), BF16)
        t = lax.broadcasted_iota(jnp.int32, (tk, tk), 0)
        c = lax.broadcasted_iota(jnp.int32, (tk, tk), 1)
        ahead = jnp.maximum(c - t, 0).astype(F32)
        diag_scr[...] = jnp.where(c // CHUNK <= t // CHUNK, (-2.0 * slope) * ahead, MASK_VALUE)

    lane = lax.broadcasted_iota(jnp.int32, (tk, V_HEAD_DIM), 1)
    aug = jnp.where(lane == 0, CHUNK * slope,
                    jnp.where(lane == 1, slope, jnp.where(lane == 2, -slope * q0.astype(F32), 0.0))).astype(BF16)
    for half in range(2):
        q = q_ref[half * tk:(half + 1) * tk, :]
        zero = jnp.zeros_like(q)
        r1, r2 = 2 * half * tk, (2 * half + 1) * tk
        qa_scr[r1:r1 + tk, :V_HEAD_DIM] = jnp.where(lane < HEAD_DIM, q, zero)
        qa_scr[r2:r2 + tk, :V_HEAD_DIM] = jnp.where(lane >= HEAD_DIM, q, zero)
        qa_scr[r1:r1 + tk, V_HEAD_DIM:] = aug
        qa_scr[r2:r2 + tk, V_HEAD_DIM:] = aug
    m_scr[...] = jnp.full_like(m_scr, MASK_VALUE)
    acc_scr[...] = jnp.zeros_like(acc_scr)

    def scores(k0, r0=0):
        return lax.dot_general(qa_scr[r0:, :], kaug_scr[pl.ds(k0, tk), :], (((1,), (1,)), ((), ())),
                               preferred_element_type=F32)

    def update(s, k0, rs=slice(None)):
        width = s.shape[1]
        m_prev = m_scr[rs, :]
        m_new = jnp.maximum(m_prev, jnp.max(s, axis=1, keepdims=True))
        alpha = jnp.exp(m_prev - m_new)
        p = jnp.exp(s - _lane_tile(m_new, width // 128))
        m_scr[rs, :] = m_new
        acc_scr[rs, :] = (acc_scr[rs, :] * _lane_tile(alpha, 2)
                          + _dot(p.astype(BF16), vaug_scr[pl.ds(k0, width), :]))

    sa_scr[...] = scores(0)

    def body(pair, carry):
        k0 = pl.multiple_of(pair * (2 * tk), 2 * tk)
        sb_scr[...] = scores(k0 + tk)
        update(sa_scr[...], k0)
        sa_scr[...] = scores(k0 + 2 * tk)
        update(sb_scr[...], k0 + tk)
        return carry

    lax.fori_loop(0, qi, body, 0)

    first, second = slice(0, 2 * tk), slice(2 * tk, rows)
    diag2 = jnp.concatenate([diag_scr[...], diag_scr[...]], axis=0)
    sb_scr[second, :] = scores(q0 + tk, 2 * tk)
    update(sa_scr[first, :] + diag2, q0, first)
    update(jnp.concatenate([sa_scr[second, :], sb_scr[second, :] + diag2], axis=1), q0, second)

    lv = lam_ref[...]
    lam = (jnp.exp(jnp.sum(lv[0:1] * lv[1:2], axis=-1, keepdims=True))
           - jnp.exp(jnp.sum(lv[2:3] * lv[3:4], axis=-1, keepdims=True)) + LAM_INIT)
    for half in range(2):
        r1, r2 = 2 * half * tk, (2 * half + 1) * tk
        a1, a2 = acc_scr[r1:r1 + tk, :], acc_scr[r2:r2 + tk, :]
        o = a1[:, :V_HEAD_DIM] / a1[:, V_HEAD_DIM:] - lam * (a2[:, :V_HEAD_DIM] / a2[:, V_HEAD_DIM:])
        y = o * lax.rsqrt(jnp.mean(o * o, axis=-1, keepdims=True) + LN_EPS)
        o_ref[half * tk:(half + 1) * tk, :] = (y * g_ref[...] * (1.0 - LAM_INIT)).astype(BF16)


def _diff_attn(mix, slopes, lam_vecs, attn_g, batch, seq):
    m = mix.shape[0]
    nq = seq // ATT_TQ
    qc, kc, vc = COL_Q // V_HEAD_DIM, COL_K // V_HEAD_DIM, COL_VAL // V_HEAD_DIM
    return pl.pallas_call(
        _attn_kernel,
        grid=(batch, N_HEADS, nq),
        in_specs=[
            pl.BlockSpec(memory_space=pltpu.SMEM),
            pl.BlockSpec((4, HEAD_DIM), lambda b, h, i: (0, 0)),
            pl.BlockSpec((ATT_TQ, V_HEAD_DIM), lambda b, h, i: (b * nq + i, qc + h)),
            pl.BlockSpec((seq, V_HEAD_DIM), lambda b, h, i: (b, kc + h)),
            pl.BlockSpec((seq, V_HEAD_DIM), lambda b, h, i: (b, vc + h)),
            pl.BlockSpec((1, V_HEAD_DIM), lambda b, h, i: (0, h)),
        ],
        out_specs=pl.BlockSpec((ATT_TQ, V_HEAD_DIM), lambda b, h, i: (b * nq + i, h)),
        out_shape=jax.ShapeDtypeStruct((m, ATTN_WIDTH), BF16),
        scratch_shapes=[
            pltpu.VMEM((seq, V_HEAD_DIM + AUG), BF16),
            pltpu.VMEM((seq, V_HEAD_DIM + AUG), BF16),
            pltpu.VMEM((ATT_TK, ATT_TK), F32),
            pltpu.VMEM((ATT_ROWS, V_HEAD_DIM + AUG), BF16),
            pltpu.VMEM((ATT_ROWS, ATT_TK), F32),
            pltpu.VMEM((ATT_ROWS, ATT_TK), F32),
            pltpu.VMEM((ATT_ROWS, 128), F32),
            pltpu.VMEM((ATT_ROWS, V_HEAD_DIM + AUG), F32),
        ],
        compiler_params=_params(("parallel", "parallel", "arbitrary")),
        name="diff_attn",
    )(slopes, lam_vecs, mix, mix, mix, attn_g)


MERGE_TM = 512
MERGE_TC = 512
MERGE_EDGE_SPLIT = 2


def _merge_kernel(xb_ref, xf_ref, ya_ref, yb_ref, wga_ref, wgb_ref, wa_ref, wb_ref, wo_ref, g_ref, b_ref,
                  o_ref, acc_scr):
    j = pl.program_id(1)
    last = pl.num_programs(1) - 1

    def partial(rows):
        xb = xb_ref[rows, :]
        merged = (jax.nn.sigmoid(_dot(xb, wga_ref[...])) * _dot(ya_ref[rows, :], wa_ref[...])
                  + jax.nn.sigmoid(_dot(xb, wgb_ref[...])) * _dot(yb_ref[rows, :], wb_ref[...]))
        return _dot(merged.astype(BF16), wo_ref[...])

    @pl.when(j == 0)
    def _():
        acc_scr[...] = partial(slice(None))

    @pl.when((j > 0) & (j < last))
    def _():
        acc_scr[...] += partial(slice(None))

    @pl.when(j == last)
    def _():
        for rows in _row_parts(MERGE_TM, MERGE_EDGE_SPLIT):
            y = ALPHA * xf_ref[rows, :] + (acc_scr[rows, :] + partial(rows))
            o_ref[rows, :] = _layer_norm(y, g_ref[...], b_ref[...])


def _merge_out(xb, xf, ya, yb, w_in, w_a, w_b, w_o, g, b):
    m, d = xf.shape
    nc = d // MERGE_TC
    ga0, gb0 = COL_GA // MERGE_TC, COL_GB // MERGE_TC
    row = lambda i, j: (i, 0)
    return pl.pallas_call(
        _merge_kernel,
        grid=(m // MERGE_TM, nc),
        in_specs=[
            pl.BlockSpec((MERGE_TM, d), row),
            pl.BlockSpec((MERGE_TM, d), row),
            pl.BlockSpec((MERGE_TM, SGU_WIDTH), row),
            pl.BlockSpec((MERGE_TM, ATTN_WIDTH), row),
            pl.BlockSpec((d, MERGE_TC), lambda i, j: (0, ga0 + j)),
            pl.BlockSpec((d, MERGE_TC), lambda i, j: (0, gb0 + j)),
            pl.BlockSpec((SGU_WIDTH, MERGE_TC), lambda i, j: (0, j)),
            pl.BlockSpec((ATTN_WIDTH, MERGE_TC), lambda i, j: (0, j)),
            pl.BlockSpec((MERGE_TC, d), lambda i, j: (j, 0)),
            pl.BlockSpec((1, d), lambda i, j: (0, 0)),
            pl.BlockSpec((1, d), lambda i, j: (0, 0)),
        ],
        out_specs=pl.BlockSpec((MERGE_TM, d), row),
        out_shape=jax.ShapeDtypeStruct((m, d), F32),
        scratch_shapes=[pltpu.VMEM((MERGE_TM, d), F32)],
        compiler_params=_params(("parallel", "arbitrary")),
        name="merge_out",
    )(xb, xf, ya, yb, w_in, w_in, w_a, w_b, w_o, g, b)


PE_TM = 512
PE_SPLIT = 2


def _pe_kernel(xb_ref, xf_ref, p_ref, wg_ref, wp_ref, g_ref, b_ref, o_ref):
    sub = PE_TM // PE_SPLIT
    for part in range(PE_SPLIT):
        rows = slice(part * sub, (part + 1) * sub)
        gate = jax.nn.sigmoid(_dot(xb_ref[rows, :], wg_ref[...]))
        emb = _dot(p_ref[rows, :].astype(BF16), wp_ref[...])
        o_ref[rows, :] = _layer_norm(ALPHA * xf_ref[rows, :] + gate * emb, g_ref[...], b_ref[...])


def _pe_gate(xb, xf, p, w_g, w_p, g, b):
    m, d = xf.shape
    row = lambda i: (i, 0)
    whole = lambda i: (0, 0)
    return pl.pallas_call(
        _pe_kernel,
        grid=(m // PE_TM,),
        in_specs=[
            pl.BlockSpec((PE_TM, d), row),
            pl.BlockSpec((PE_TM, d), row),
            pl.BlockSpec((PE_TM, P_DIM), row),
            pl.BlockSpec((d, d), whole),
            pl.BlockSpec((P_DIM, d), whole),
            pl.BlockSpec((1, d), whole),
            pl.BlockSpec((1, d), whole),
        ],
        out_specs=pl.BlockSpec((PE_TM, d), row),
        out_shape=jax.ShapeDtypeStruct((m, d), F32),
        compiler_params=_params(("parallel",)),
        name="pe_gate",
    )(xb, xf, p, w_g, w_p, g, b)


def kernel(x, p, ffn1_w_gu, ffn1_w_down, ln1_g, ln1_b, w_in, sgu_ln_g, sgu_ln_b, sgu_w, sgu_b, lam_q1, lam_k1, lam_q2, lam_k2, attn_norm_g, w_branch_a, w_branch_b, w_out, ln2_g, ln2_b, ffn2_w_gu, ffn2_w_down, ln3_g, ln3_b, w_pe_gate, w_pe_proj, ln4_g, ln4_b):
    batch, seq, d = x.shape
    m = batch * seq
    slopes = jnp.asarray(2.0 ** (-8.0 * np.arange(1, N_HEADS + 1) / N_HEADS), dtype=F32)
    xf = x.reshape(m, d)
    for i in range(DEPTH):
        w_in_b = w_in[i].astype(BF16)
        bias_full = jnp.repeat(sgu_b[i].T, SGU_BLOCK, axis=1)
        lam_vecs = jnp.stack([lam_q1[i], lam_k1[i], lam_q2[i], lam_k2[i]]).astype(F32)

        x1f, x1b = _ffn_ln(xf, ffn1_w_gu[i].astype(BF16), ffn1_w_down[i].astype(BF16),
                           ln1_g[i][None], ln1_b[i][None])
        mix = _in_proj(x1b, w_in_b, sgu_ln_g[i][None], sgu_ln_b[i][None])
        y_a = _sgu(mix, sgu_w[i], bias_full)
        y_b = _diff_attn(mix, slopes, lam_vecs, attn_norm_g[i][None], batch, seq)
        x2f = _merge_out(x1b, x1f, y_a, y_b, w_in_b, w_branch_a[i].astype(BF16), w_branch_b[i].astype(BF16),
                         w_out[i].astype(BF16), ln2_g[i][None], ln2_b[i][None])
        x3f, x3b = _ffn_ln(x2f, ffn2_w_gu[i].astype(BF16), ffn2_w_down[i].astype(BF16),
                           ln3_g[i][None], ln3_b[i][None])
        xf = _pe_gate(x3b, x3f, p[i].reshape(m, P_DIM), w_pe_gate[i].astype(BF16), w_pe_proj[i].astype(BF16),
                      ln4_g[i][None], ln4_b[i][None])
    return xf.reshape(batch, seq, d)
```

```python
import math

import jax
import jax.numpy as jnp
import numpy as np
from jax import lax
from jax.experimental import pallas as pl
from jax.experimental.pallas import tpu as pltpu

CHUNK = 64
P_DIM = 256
SGU_BLOCK = 128
SGU_GROUPS = 8
SGU_WIDTH = 1024
N_HEADS = 8
HEAD_DIM = 64
V_HEAD_DIM = 128
ATTN_WIDTH = 1024
DEPTH = 1
ALPHA = (2 * DEPTH) ** 0.25
LN_EPS = 1e-5
LAM_INIT = 0.8 - 0.6 * math.exp(-0.3 * 0)

COL_U, COL_V, COL_Q, COL_K, COL_VAL = 0, 1024, 2048, 3072, 4096
COL_GA, COL_GB = 5120, 7168
MIX_COLS = 5120

BF16 = jnp.bfloat16
F32 = jnp.float32

VMEM_LIMIT_BYTES = 56 * 1024 * 1024

MASK_VALUE = -1e30


def _params(semantics):
    return pltpu.CompilerParams(dimension_semantics=semantics, vmem_limit_bytes=VMEM_LIMIT_BYTES)


def _dot(a, b):
    return jnp.dot(a, b, preferred_element_type=F32)


def _layer_norm(y, g, b):
    mu = jnp.mean(y, axis=-1, keepdims=True)
    d = y - mu
    var = jnp.mean(d * d, axis=-1, keepdims=True)
    return d * lax.rsqrt(var + LN_EPS) * g + b


FFN_TM = 1024
FFN_TF = 512
FFN_EDGE_SPLIT = 2


def _row_parts(n_rows, parts):
    size = n_rows // parts
    return [slice(k * size, (k + 1) * size) for k in range(parts)]


def _ffn_ln_kernel(x_ref, wg_ref, wu_ref, wd_ref, g_ref, b_ref, o_ref, xb_scr):
    j = pl.program_id(1)
    last = pl.num_programs(1) - 1

    def partial(rows):
        xb = xb_scr[rows, :]
        gate = _dot(xb, wg_ref[...])
        up = _dot(xb, wu_ref[...])
        h = (gate * jax.nn.sigmoid(gate)) * up
        return _dot(h.astype(BF16), wd_ref[...])

    @pl.when(j == 0)
    def _():
        for rows in _row_parts(FFN_TM, FFN_EDGE_SPLIT):
            xb_scr[rows, :] = x_ref[rows, :].astype(BF16)
            o_ref[rows, :] = partial(rows)

    @pl.when((j > 0) & (j < last))
    def _():
        o_ref[...] += partial(slice(None))

    @pl.when(j == last)
    def _():
        for rows in _row_parts(FFN_TM, FFN_EDGE_SPLIT):
            y = ALPHA * x_ref[rows, :] + 0.5 * (o_ref[rows, :] + partial(rows))
            o_ref[rows, :] = _layer_norm(y, g_ref[...], b_ref[...])


def _ffn_ln(x, w_gu, w_down, g, b):
    m, d = x.shape
    f = w_down.shape[0]
    nf = f // FFN_TF
    return pl.pallas_call(
        _ffn_ln_kernel,
        grid=(m // FFN_TM, nf),
        in_specs=[
            pl.BlockSpec((FFN_TM, d), lambda i, j: (i, 0)),
            pl.BlockSpec((d, FFN_TF), lambda i, j: (0, j)),
            pl.BlockSpec((d, FFN_TF), lambda i, j: (0, nf + j)),
            pl.BlockSpec((FFN_TF, d), lambda i, j: (j, 0)),
            pl.BlockSpec((1, d), lambda i, j: (0, 0)),
            pl.BlockSpec((1, d), lambda i, j: (0, 0)),
        ],
        out_specs=pl.BlockSpec((FFN_TM, d), lambda i, j: (i, 0)),
        out_shape=jax.ShapeDtypeStruct((m, d), F32),
        scratch_shapes=[pltpu.VMEM((FFN_TM, d), BF16)],
        compiler_params=_params(("parallel", "arbitrary")),
        name="ffn_ln",
    )(x, w_gu, w_gu, w_down, g, b)


PROJ_TM = 1024
PROJ_TN = 1024


def _in_proj_kernel(x_ref, w_ref, g_ref, b_ref, o_ref, xb_scr):
    j = pl.program_id(1)

    def proj(rows=slice(None)):
        return _dot(xb_scr[rows, :], w_ref[...])

    @pl.when(j == COL_U // PROJ_TN)
    def _():
        for rows in _row_parts(PROJ_TM, 2):
            xb_scr[rows, :] = x_ref[rows, :].astype(BF16)
            o_ref[rows, :] = jax.nn.gelu(proj(rows)).astype(BF16)

    @pl.when(j == COL_V // PROJ_TN)
    def _():
        o_ref[...] = _layer_norm(jax.nn.gelu(proj()), g_ref[...], b_ref[...]).astype(BF16)

    @pl.when(j == COL_Q // PROJ_TN)
    def _():
        o_ref[...] = (proj() * HEAD_DIM ** -0.5).astype(BF16)

    @pl.when(j >= COL_K // PROJ_TN)
    def _():
        o_ref[...] = proj().astype(BF16)


def _in_proj(x, w_in, sgu_g, sgu_b):
    m, d = x.shape
    return pl.pallas_call(
        _in_proj_kernel,
        grid=(m // PROJ_TM, MIX_COLS // PROJ_TN),
        in_specs=[
            pl.BlockSpec((PROJ_TM, d), lambda i, j: (i, 0)),
            pl.BlockSpec((d, PROJ_TN), lambda i, j: (0, j)),
            pl.BlockSpec((1, SGU_WIDTH), lambda i, j: (0, 0)),
            pl.BlockSpec((1, SGU_WIDTH), lambda i, j: (0, 0)),
        ],
        out_specs=pl.BlockSpec((PROJ_TM, PROJ_TN), lambda i, j: (i, j)),
        out_shape=jax.ShapeDtypeStruct((m, MIX_COLS), BF16),
        scratch_shapes=[pltpu.VMEM((PROJ_TM, d), BF16)],
        compiler_params=_params(("parallel", "arbitrary")),
        name="in_proj",
    )(x, w_in, sgu_g, sgu_b)


SGU_TM = 512


def _sgu_kernel(u_ref, v_ref, w_ref, bias_ref, o_ref):
    t = lax.broadcasted_iota(jnp.int32, (SGU_BLOCK, SGU_BLOCK), 0)
    s = lax.broadcasted_iota(jnp.int32, (SGU_BLOCK, SGU_BLOCK), 1)
    allowed = (s // CHUNK) <= (t // CHUNK)
    for g in range(SGU_GROUPS):
        wm = jnp.where(allowed, w_ref[g], 0.0).astype(BF16)
        cols = slice(g * SGU_BLOCK, (g + 1) * SGU_BLOCK)
        bias = bias_ref[:, cols]
        for blk in range(SGU_TM // SGU_BLOCK):
            rows = slice(blk * SGU_BLOCK, (blk + 1) * SGU_BLOCK)
            mix = _dot(wm, v_ref[rows, cols]) + bias
            o_ref[rows, cols] = (u_ref[rows, cols].astype(F32) * mix).astype(BF16)


def _sgu(mix, sgu_w, bias_full):
    m = mix.shape[0]
    return pl.pallas_call(
        _sgu_kernel,
        grid=(m // SGU_TM,),
        in_specs=[
            pl.BlockSpec((SGU_TM, SGU_WIDTH), lambda i: (i, COL_U // SGU_WIDTH)),
            pl.BlockSpec((SGU_TM, SGU_WIDTH), lambda i: (i, COL_V // SGU_WIDTH)),
            pl.BlockSpec((SGU_GROUPS, SGU_BLOCK, SGU_BLOCK), lambda i: (0, 0, 0)),
            pl.BlockSpec((SGU_BLOCK, SGU_WIDTH), lambda i: (0, 0)),
        ],
        out_specs=pl.BlockSpec((SGU_TM, SGU_WIDTH), lambda i: (i, 0)),
        out_shape=jax.ShapeDtypeStruct((m, SGU_WIDTH), BF16),
        compiler_params=_params(("parallel",)),
        name="sgu",
    )(mix, mix, sgu_w, bias_full)


ATT_TK = 512
ATT_TQ = 2 * ATT_TK
ATT_ROWS = 2 * ATT_TQ
AUG = 128


def _lane_tile(x, n):
    return jnp.concatenate([x] * n, axis=1)


def _attn_kernel(slopes_ref, lam_ref, q_ref, k_ref, v_ref, g_ref, o_ref,
                 kaug_scr, vaug_scr, diag_scr, qa_scr, sa_scr, sb_scr, m_scr, acc_scr):
    h = pl.program_id(1)
    qi = pl.program_id(2)
    tq, tk, rows = ATT_TQ, ATT_TK, ATT_ROWS
    seq = k_ref.shape[0]
    slope = slopes_ref[h]
    q0 = pl.multiple_of(qi * tq, tq)

    @pl.when(qi == 0)
    def _():
        pos = lax.broadcasted_iota(jnp.int32, (seq, AUG), 0)
        lane = lax.broadcasted_iota(jnp.int32, (seq, AUG), 1)
        feat = jnp.where(lane == 0, pos // CHUNK, jnp.where(lane == 1, pos % CHUNK, jnp.where(lane == 2, 1, 0)))
        kaug_scr[:, :V_HEAD_DIM] = k_ref[...]
        kaug_scr[:, V_HEAD_DIM:] = feat.astype(F32).astype(BF16)
        vaug_scr[:, :V_HEAD_DIM] = v_ref[...]
        vaug_scr[:, V_HEAD_DIM:] = jnp.ones((seq, AUG), BF16)
        t = lax.broadcasted_iota(jnp.int32, (tk, tk), 0)
        c = lax.broadcasted_iota(jnp.int32, (tk, tk), 1)
        ahead = jnp.maximum(c - t, 0).astype(F32)
        diag_scr[...] = jnp.where(c // CHUNK <= t // CHUNK, (-2.0 * slope) * ahead, MASK_VALUE)

    lane = lax.broadcasted_iota(jnp.int32, (tk, V_HEAD_DIM), 1)
    aug = jnp.where(lane == 0, CHUNK * slope,
                    jnp.where(lane == 1, slope, jnp.where(lane == 2, -slope * q0.astype(F32), 0.0))).astype(BF16)
    for half in range(2):
        q = q_ref[half * tk:(half + 1) * tk, :]
        zero = jnp.zeros_like(q)
        r1, r2 = 2 * half * tk, (2 * half + 1) * tk
        qa_scr[r1:r1 + tk, :V_HEAD_DIM] = jnp.where(lane < HEAD_DIM, q, zero)
        qa_scr[r2:r2 + tk, :V_HEAD_DIM] = jnp.where(lane >= HEAD_DIM, q, zero)
        qa_scr[r1:r1 + tk, V_HEAD_DIM:] = aug
        qa_scr[r2:r2 + tk, V_HEAD_DIM:] = aug
    m_scr[...] = jnp.full_like(m_scr, MASK_VALUE)
    acc_scr[...] = jnp.zeros_like(acc_scr)

    def scores(k0, r0=0):
        return lax.dot_general(qa_scr[r0:, :], kaug_scr[pl.ds(k0, tk), :], (((1,), (1,)), ((), ())),
                               preferred_element_type=F32)

    def update(s, k0, rs=slice(None)):
        width = s.shape[1]
        m_prev = m_scr[rs, :]
        m_new = jnp.maximum(m_prev, jnp.max(s, axis=1, keepdims=True))
        alpha = jnp.exp(m_prev - m_new)
        p = jnp.exp(s - _lane_tile(m_new, width // 128))
        m_scr[rs, :] = m_new
        acc_scr[rs, :] = (acc_scr[rs, :] * _lane_tile(alpha, 2)
                          + _dot(p.astype(BF16), vaug_scr[pl.ds(k0, width), :]))

    sa_scr[...] = scores(0)

    def body(pair, carry):
        k0 = pl.multiple_of(pair * (2 * tk), 2 * tk)
        sb_scr[...] = scores(k0 + tk)
        update(sa_scr[...], k0)
        sa_scr[...] = scores(k0 + 2 * tk)
        update(sb_scr[...], k0 + tk)
        return carry

    lax.fori_loop(0, qi, body, 0)

    first, second = slice(0, 2 * tk), slice(2 * tk, rows)
    diag2 = jnp.concatenate([diag_scr[...], diag_scr[...]], axis=0)
    sb_scr[second, :] = scores(q0 + tk, 2 * tk)
    update(sa_scr[first, :] + diag2, q0, first)
    update(jnp.concatenate([sa_scr[second, :], sb_scr[second, :] + diag2], axis=1), q0, second)

    lv = lam_ref[...]
    lam = (jnp.exp(jnp.sum(lv[0:1] * lv[1:2], axis=-1, keepdims=True))
           - jnp.exp(jnp.sum(lv[2:3] * lv[3:4], axis=-1, keepdims=True)) + LAM_INIT)
    for half in range(2):
        r1, r2 = 2 * half * tk, (2 * half + 1) * tk
        a1, a2 = acc_scr[r1:r1 + tk, :], acc_scr[r2:r2 + tk, :]
        o = a1[:, :V_HEAD_DIM] / a1[:, V_HEAD_DIM:] - lam * (a2[:, :V_HEAD_DIM] / a2[:, V_HEAD_DIM:])
        y = o * lax.rsqrt(jnp.mean(o * o, axis=-1, keepdims=True) + LN_EPS)
        o_ref[half * tk:(half + 1) * tk, :] = (y * g_ref[...] * (1.0 - LAM_INIT)).astype(BF16)


def _diff_attn(mix, slopes, lam_vecs, attn_g, batch, seq):
    m = mix.shape[0]
    nq = seq // ATT_TQ
    qc, kc, vc = COL_Q // V_HEAD_DIM, COL_K // V_HEAD_DIM, COL_VAL // V_HEAD_DIM
    return pl.pallas_call(
        _attn_kernel,
        grid=(batch, N_HEADS, nq),
        in_specs=[
            pl.BlockSpec(memory_space=pltpu.SMEM),
            pl.BlockSpec((4, HEAD_DIM), lambda b, h, i: (0, 0)),
            pl.BlockSpec((ATT_TQ, V_HEAD_DIM), lambda b, h, i: (b * nq + i, qc + h)),
            pl.BlockSpec((seq, V_HEAD_DIM), lambda b, h, i: (b, kc + h)),
            pl.BlockSpec((seq, V_HEAD_DIM), lambda b, h, i: (b, vc + h)),
            pl.BlockSpec((1, V_HEAD_DIM), lambda b, h, i: (0, h)),
        ],
        out_specs=pl.BlockSpec((ATT_TQ, V_HEAD_DIM), lambda b, h, i: (b * nq + i, h)),
        out_shape=jax.ShapeDtypeStruct((m, ATTN_WIDTH), BF16),
        scratch_shapes=[
            pltpu.VMEM((seq, V_HEAD_DIM + AUG), BF16),
            pltpu.VMEM((seq, V_HEAD_DIM + AUG), BF16),
            pltpu.VMEM((ATT_TK, ATT_TK), F32),
            pltpu.VMEM((ATT_ROWS, V_HEAD_DIM + AUG), BF16),
            pltpu.VMEM((ATT_ROWS, ATT_TK), F32),
            pltpu.VMEM((ATT_ROWS, ATT_TK), F32),
            pltpu.VMEM((ATT_ROWS, 128), F32),
            pltpu.VMEM((ATT_ROWS, V_HEAD_DIM + AUG), F32),
        ],
        compiler_params=_params(("parallel", "parallel", "arbitrary")),
        name="diff_attn",
    )(slopes, lam_vecs, mix, mix, mix, attn_g)


MERGE_TM = 512
MERGE_TC = 512
MERGE_EDGE_SPLIT = 2


def _merge_kernel(xf_ref, ya_ref, yb_ref, wga_ref, wgb_ref, wa_ref, wb_ref, wo_ref, g_ref, b_ref,
                  o_ref, xb_scr, acc_scr):
    j = pl.program_id(1)
    last = pl.num_programs(1) - 1

    def partial(rows):
        xb = xb_scr[rows, :]
        merged = (jax.nn.sigmoid(_dot(xb, wga_ref[...])) * _dot(ya_ref[rows, :], wa_ref[...])
                  + jax.nn.sigmoid(_dot(xb, wgb_ref[...])) * _dot(yb_ref[rows, :], wb_ref[...]))
        return _dot(merged.astype(BF16), wo_ref[...])

    @pl.when(j == 0)
    def _():
        for rows in _row_parts(MERGE_TM, MERGE_EDGE_SPLIT):
            xb_scr[rows, :] = xf_ref[rows, :].astype(BF16)
            acc_scr[rows, :] = partial(rows)

    @pl.when((j > 0) & (j < last))
    def _():
        acc_scr[...] += partial(slice(None))

    @pl.when(j == last)
    def _():
        for rows in _row_parts(MERGE_TM, MERGE_EDGE_SPLIT):
            y = ALPHA * xf_ref[rows, :] + (acc_scr[rows, :] + partial(rows))
            o_ref[rows, :] = _layer_norm(y, g_ref[...], b_ref[...])


def _merge_out(xf, ya, yb, w_in, w_a, w_b, w_o, g, b):
    m, d = xf.shape
    nc = d // MERGE_TC
    ga0, gb0 = COL_GA // MERGE_TC, COL_GB // MERGE_TC
    row = lambda i, j: (i, 0)
    return pl.pallas_call(
        _merge_kernel,
        grid=(m // MERGE_TM, nc),
        in_specs=[
            pl.BlockSpec((MERGE_TM, d), row),
            pl.BlockSpec((MERGE_TM, SGU_WIDTH), row),
            pl.BlockSpec((MERGE_TM, ATTN_WIDTH), row),
            pl.BlockSpec((d, MERGE_TC), lambda i, j: (0, ga0 + j)),
            pl.BlockSpec((d, MERGE_TC), lambda i, j: (0, gb0 + j)),
            pl.BlockSpec((SGU_WIDTH, MERGE_TC), lambda i, j: (0, j)),
            pl.BlockSpec((ATTN_WIDTH, MERGE_TC), lambda i, j: (0, j)),
            pl.BlockSpec((MERGE_TC, d), lambda i, j: (j, 0)),
            pl.BlockSpec((1, d), lambda i, j: (0, 0)),
            pl.BlockSpec((1, d), lambda i, j: (0, 0)),
        ],
        out_specs=pl.BlockSpec((MERGE_TM, d), row),
        out_shape=jax.ShapeDtypeStruct((m, d), F32),
        scratch_shapes=[pltpu.VMEM((MERGE_TM, d), BF16), pltpu.VMEM((MERGE_TM, d), F32)],
        compiler_params=_params(("parallel", "arbitrary")),
        name="merge_out",
    )(xf, ya, yb, w_in, w_in, w_a, w_b, w_o, g, b)


PE_TM = 512
PE_SPLIT = 2


def _pe_kernel(xf_ref, p_ref, wg_ref, wp_ref, g_ref, b_ref, o_ref):
    for rows in _row_parts(PE_TM, PE_SPLIT):
        x = xf_ref[rows, :]
        gate = jax.nn.sigmoid(_dot(x.astype(BF16), wg_ref[...]))
        emb = _dot(p_ref[rows, :].astype(BF16), wp_ref[...])
        o_ref[rows, :] = _layer_norm(ALPHA * x + gate * emb, g_ref[...], b_ref[...])


def _pe_gate(xf, p, w_g, w_p, g, b):
    m, d = xf.shape
    row = lambda i: (i, 0)
    whole = lambda i: (0, 0)
    return pl.pallas_call(
        _pe_kernel,
        grid=(m // PE_TM,),
        in_specs=[
            pl.BlockSpec((PE_TM, d), row),
            pl.BlockSpec((PE_TM, P_DIM), row),
            pl.BlockSpec((d, d), whole),
            pl.BlockSpec((P_DIM, d), whole),
            pl.BlockSpec((1, d), whole),
            pl.BlockSpec((1, d), whole),
        ],
        out_specs=pl.BlockSpec((PE_TM, d), row),
        out_shape=jax.ShapeDtypeStruct((m, d), F32),
        compiler_params=_params(("parallel",)),
        name="pe_gate",
    )(xf, p, w_g, w_p, g, b)


def kernel(x, p, ffn1_w_gu, ffn1_w_down, ln1_g, ln1_b, w_in, sgu_ln_g, sgu_ln_b, sgu_w, sgu_b, lam_q1, lam_k1, lam_q2, lam_k2, attn_norm_g, w_branch_a, w_branch_b, w_out, ln2_g, ln2_b, ffn2_w_gu, ffn2_w_down, ln3_g, ln3_b, w_pe_gate, w_pe_proj, ln4_g, ln4_b):
    batch, seq, d = x.shape
    m = batch * seq
    slopes = jnp.asarray(2.0 ** (-8.0 * np.arange(1, N_HEADS + 1) / N_HEADS), dtype=F32)
    xf = x.reshape(m, d)
    for i in range(DEPTH):
        w_in_b = w_in[i].astype(BF16)
        bias_full = jnp.repeat(sgu_b[i].T, SGU_BLOCK, axis=1)
        lam_vecs = jnp.stack([lam_q1[i], lam_k1[i], lam_q2[i], lam_k2[i]]).astype(F32)

        x1 = _ffn_ln(xf, ffn1_w_gu[i].astype(BF16), ffn1_w_down[i].astype(BF16), ln1_g[i][None], ln1_b[i][None])
        mix = _in_proj(x1, w_in_b, sgu_ln_g[i][None], sgu_ln_b[i][None])
        y_a = _sgu(mix, sgu_w[i], bias_full)
        y_b = _diff_attn(mix, slopes, lam_vecs, attn_norm_g[i][None], batch, seq)
        x2 = _merge_out(x1, y_a, y_b, w_in_b, w_branch_a[i].astype(BF16), w_branch_b[i].astype(BF16),
                        w_out[i].astype(BF16), ln2_g[i][None], ln2_b[i][None])
        x3 = _ffn_ln(x2, ffn2_w_gu[i].astype(BF16), ffn2_w_down[i].astype(BF16), ln3_g[i][None], ln3_b[i][None])
        xf = _pe_gate(x3, p[i].reshape(m, P_DIM), w_pe_gate[i].astype(BF16), w_pe_proj[i].astype(BF16),
                      ln4_g[i][None], ln4_b[i][None])
    return xf.reshape(batch, seq, d)
```

```python
import math

import jax
import jax.numpy as jnp
import numpy as np
from jax import lax
from jax.experimental import pallas as pl
from jax.experimental.pallas import tpu as pltpu

CHUNK = 64
P_DIM = 256
SGU_BLOCK = 128
SGU_GROUPS = 8
SGU_WIDTH = 1024
N_HEADS = 8
HEAD_DIM = 64
V_HEAD_DIM = 128
ATTN_WIDTH = 1024
DEPTH = 1
ALPHA = (2 * DEPTH) ** 0.25
LN_EPS = 1e-5
LAM_INIT = 0.8 - 0.6 * math.exp(-0.3 * 0)

COL_U, COL_V, COL_Q, COL_K, COL_VAL = 0, 1024, 2048, 3072, 4096
COL_GA, COL_GB = 5120, 7168
MIX_COLS = 5120

BF16 = jnp.bfloat16
F32 = jnp.float32

VMEM_LIMIT_BYTES = 60 * 1024 * 1024

MASK_VALUE = -1e30


def _params(semantics):
    return pltpu.CompilerParams(dimension_semantics=semantics, vmem_limit_bytes=VMEM_LIMIT_BYTES)


def _dot(a, b):
    return jnp.dot(a, b, preferred_element_type=F32)


def _layer_norm(y, g, b):
    mu = jnp.mean(y, axis=-1, keepdims=True)
    d = y - mu
    var = jnp.mean(d * d, axis=-1, keepdims=True)
    return d * lax.rsqrt(var + LN_EPS) * g + b


FFN_TM = 1024
FFN_TF = 512
FFN_EDGE_SPLIT = 2


def _row_parts(n_rows, parts):
    size = n_rows // parts
    return [slice(k * size, (k + 1) * size) for k in range(parts)]


def _ffn_ln_kernel(x_ref, wg_ref, wu_ref, wd_ref, g_ref, b_ref, o_ref, xb_scr):
    j = pl.program_id(1)
    last = pl.num_programs(1) - 1

    def partial(rows, wd):
        xb = xb_scr[rows, :]
        gate = _dot(xb, wg_ref[...])
        up = _dot(xb, wu_ref[...])
        h = (gate * jax.nn.sigmoid(gate)) * up
        return _dot(h.astype(BF16), wd)

    def wd_chunk():
        return wd_ref[...].astype(BF16)

    @pl.when(j == 0)
    def _():
        wd = wd_chunk()
        for rows in _row_parts(FFN_TM, FFN_EDGE_SPLIT):
            xb_scr[rows, :] = x_ref[rows, :].astype(BF16)
            o_ref[rows, :] = partial(rows, wd)

    @pl.when((j > 0) & (j < last))
    def _():
        o_ref[...] += partial(slice(None), wd_chunk())

    @pl.when(j == last)
    def _():
        wd = wd_chunk()
        for rows in _row_parts(FFN_TM, FFN_EDGE_SPLIT):
            y = ALPHA * x_ref[rows, :] + 0.5 * (o_ref[rows, :] + partial(rows, wd))
            o_ref[rows, :] = _layer_norm(y, g_ref[...], b_ref[...])


def _ffn_ln(x, w_gu, w_down, g, b):
    m, d = x.shape
    f = w_down.shape[0]
    nf = f // FFN_TF
    return pl.pallas_call(
        _ffn_ln_kernel,
        grid=(m // FFN_TM, nf),
        in_specs=[
            pl.BlockSpec((FFN_TM, d), lambda i, j: (i, 0)),
            pl.BlockSpec((d, FFN_TF), lambda i, j: (0, j)),
            pl.BlockSpec((d, FFN_TF), lambda i, j: (0, nf + j)),
            pl.BlockSpec((FFN_TF, d), lambda i, j: (j, 0)),
            pl.BlockSpec((1, d), lambda i, j: (0, 0)),
            pl.BlockSpec((1, d), lambda i, j: (0, 0)),
        ],
        out_specs=pl.BlockSpec((FFN_TM, d), lambda i, j: (i, 0)),
        out_shape=jax.ShapeDtypeStruct((m, d), F32),
        scratch_shapes=[pltpu.VMEM((FFN_TM, d), BF16)],
        compiler_params=_params(("parallel", "arbitrary")),
        name="ffn_ln",
    )(x, w_gu, w_gu, w_down, g, b)


PROJ_TM = 1024
PROJ_TN = 1024
PROJ_EDGE_SPLIT = 2


def _in_proj_kernel(x_ref, w_ref, g_ref, b_ref, o_ref, xb_scr):
    j = pl.program_id(1)

    def proj(rows=slice(None)):
        return _dot(xb_scr[rows, :], w_ref[...])

    @pl.when(j == COL_U // PROJ_TN)
    def _():
        for rows in _row_parts(PROJ_TM, PROJ_EDGE_SPLIT):
            xb_scr[rows, :] = x_ref[rows, :].astype(BF16)
            o_ref[rows, :] = jax.nn.gelu(proj(rows)).astype(BF16)

    @pl.when(j == COL_V // PROJ_TN)
    def _():
        o_ref[...] = _layer_norm(jax.nn.gelu(proj()), g_ref[...], b_ref[...]).astype(BF16)

    @pl.when(j == COL_Q // PROJ_TN)
    def _():
        o_ref[...] = (proj() * HEAD_DIM ** -0.5).astype(BF16)

    @pl.when(j >= COL_K // PROJ_TN)
    def _():
        o_ref[...] = proj().astype(BF16)


def _in_proj(x, w_in, sgu_g, sgu_b):
    m, d = x.shape
    return pl.pallas_call(
        _in_proj_kernel,
        grid=(m // PROJ_TM, MIX_COLS // PROJ_TN),
        in_specs=[
            pl.BlockSpec((PROJ_TM, d), lambda i, j: (i, 0)),
            pl.BlockSpec((d, PROJ_TN), lambda i, j: (0, j)),
            pl.BlockSpec((1, SGU_WIDTH), lambda i, j: (0, 0)),
            pl.BlockSpec((1, SGU_WIDTH), lambda i, j: (0, 0)),
        ],
        out_specs=pl.BlockSpec((PROJ_TM, PROJ_TN), lambda i, j: (i, j)),
        out_shape=jax.ShapeDtypeStruct((m, MIX_COLS), BF16),
        scratch_shapes=[pltpu.VMEM((PROJ_TM, d), BF16)],
        compiler_params=_params(("parallel", "arbitrary")),
        name="in_proj",
    )(x, w_in, sgu_g, sgu_b)


SGU_TM = 512


def _sgu_kernel(u_ref, v_ref, w_ref, bias_ref, o_ref):
    t = lax.broadcasted_iota(jnp.int32, (SGU_BLOCK, SGU_BLOCK), 0)
    s = lax.broadcasted_iota(jnp.int32, (SGU_BLOCK, SGU_BLOCK), 1)
    allowed = (s // CHUNK) <= (t // CHUNK)
    for g in range(SGU_GROUPS):
        wm = jnp.where(allowed, w_ref[g], 0.0).astype(BF16)
        cols = slice(g * SGU_BLOCK, (g + 1) * SGU_BLOCK)
        bias = bias_ref[:, cols]
        for blk in range(SGU_TM // SGU_BLOCK):
            rows = slice(blk * SGU_BLOCK, (blk + 1) * SGU_BLOCK)
            mix = _dot(wm, v_ref[rows, cols]) + bias
            o_ref[rows, cols] = (u_ref[rows, cols].astype(F32) * mix).astype(BF16)


def _sgu(mix, sgu_w, bias_full):
    m = mix.shape[0]
    return pl.pallas_call(
        _sgu_kernel,
        grid=(m // SGU_TM,),
        in_specs=[
            pl.BlockSpec((SGU_TM, SGU_WIDTH), lambda i: (i, COL_U // SGU_WIDTH)),
            pl.BlockSpec((SGU_TM, SGU_WIDTH), lambda i: (i, COL_V // SGU_WIDTH)),
            pl.BlockSpec((SGU_GROUPS, SGU_BLOCK, SGU_BLOCK), lambda i: (0, 0, 0)),
            pl.BlockSpec((SGU_BLOCK, SGU_WIDTH), lambda i: (0, 0)),
        ],
        out_specs=pl.BlockSpec((SGU_TM, SGU_WIDTH), lambda i: (i, 0)),
        out_shape=jax.ShapeDtypeStruct((m, SGU_WIDTH), BF16),
        compiler_params=_params(("parallel",)),
        name="sgu",
    )(mix, mix, sgu_w, bias_full)


ATT_TK = 512
ATT_TQ = 2 * ATT_TK
ATT_ROWS = 2 * ATT_TQ
AUG = 128


def _lane_tile(x, n):
    return jnp.concatenate([x] * n, axis=1)


def _attn_kernel(slopes_ref, lam_ref, q_ref, k_ref, v_ref, g_ref, o_ref,
                 kaug_scr, vaug_scr, diag_scr, qa_scr, sa_scr, sb_scr, m_scr, acc_scr):
    h = pl.program_id(1)
    qi = pl.program_id(2)
    tq, tk, rows = ATT_TQ, ATT_TK, ATT_ROWS
    seq = k_ref.shape[0]
    slope = slopes_ref[h]
    q0 = pl.multiple_of(qi * tq, tq)

    @pl.when(qi == 0)
    def _():
        pos = lax.broadcasted_iota(jnp.int32, (seq, AUG), 0)
        lane = lax.broadcasted_iota(jnp.int32, (seq, AUG), 1)
        feat = jnp.where(lane == 0, pos // CHUNK, jnp.where(lane == 1, pos % CHUNK, jnp.where(lane == 2, 1, 0)))
        kaug_scr[:, :V_HEAD_DIM] = k_ref[...]
        kaug_scr[:, V_HEAD_DIM:] = feat.astype(F32).astype(BF16)
        vaug_scr[:, :V_HEAD_DIM] = v_ref[...]
        vaug_scr[:, V_HEAD_DIM:] = jnp.ones((seq, AUG), BF16)
        t = lax.broadcasted_iota(jnp.int32, (tk, tk), 0)
        c = lax.broadcasted_iota(jnp.int32, (tk, tk), 1)
        ahead = jnp.maximum(c - t, 0).astype(F32)
        diag_scr[...] = jnp.where(c // CHUNK <= t // CHUNK, (-2.0 * slope) * ahead, MASK_VALUE)

    lane = lax.broadcasted_iota(jnp.int32, (tk, V_HEAD_DIM), 1)
    aug = jnp.where(lane == 0, CHUNK * slope,
                    jnp.where(lane == 1, slope, jnp.where(lane == 2, -slope * q0.astype(F32), 0.0))).astype(BF16)
    for half in range(2):
        q = q_ref[half * tk:(half + 1) * tk, :]
        zero = jnp.zeros_like(q)
        r1, r2 = 2 * half * tk, (2 * half + 1) * tk
        qa_scr[r1:r1 + tk, :V_HEAD_DIM] = jnp.where(lane < HEAD_DIM, q, zero)
        qa_scr[r2:r2 + tk, :V_HEAD_DIM] = jnp.where(lane >= HEAD_DIM, q, zero)
        qa_scr[r1:r1 + tk, V_HEAD_DIM:] = aug
        qa_scr[r2:r2 + tk, V_HEAD_DIM:] = aug
    m_scr[...] = jnp.full_like(m_scr, MASK_VALUE)
    acc_scr[...] = jnp.zeros_like(acc_scr)

    def scores(k0, r0=0):
        return lax.dot_general(qa_scr[r0:, :], kaug_scr[pl.ds(k0, tk), :], (((1,), (1,)), ((), ())),
                               preferred_element_type=F32)

    def update(s, k0, rs=slice(None)):
        width = s.shape[1]
        m_prev = m_scr[rs, :]
        m_new = jnp.maximum(m_prev, jnp.max(s, axis=1, keepdims=True))
        alpha = jnp.exp(m_prev - m_new)
        p = jnp.exp(s - _lane_tile(m_new, width // 128))
        m_scr[rs, :] = m_new
        acc_scr[rs, :] = (acc_scr[rs, :] * _lane_tile(alpha, 2)
                          + _dot(p.astype(BF16), vaug_scr[pl.ds(k0, width), :]))

    sa_scr[...] = scores(0)

    def body(pair, carry):
        k0 = pl.multiple_of(pair * (2 * tk), 2 * tk)
        sb_scr[...] = scores(k0 + tk)
        update(sa_scr[...], k0)
        sa_scr[...] = scores(k0 + 2 * tk)
        update(sb_scr[...], k0 + tk)
        return carry

    lax.fori_loop(0, qi, body, 0)

    first, second = slice(0, 2 * tk), slice(2 * tk, rows)
    diag2 = jnp.concatenate([diag_scr[...], diag_scr[...]], axis=0)
    sb_scr[second, :] = scores(q0 + tk, 2 * tk)
    update(sa_scr[first, :] + diag2, q0, first)
    update(jnp.concatenate([sa_scr[second, :], sb_scr[second, :] + diag2], axis=1), q0, second)

    lv = lam_ref[...]
    lam = (jnp.exp(jnp.sum(lv[0:1] * lv[1:2], axis=-1, keepdims=True))
           - jnp.exp(jnp.sum(lv[2:3] * lv[3:4], axis=-1, keepdims=True)) + LAM_INIT)
    for half in range(2):
        r1, r2 = 2 * half * tk, (2 * half + 1) * tk
        a1, a2 = acc_scr[r1:r1 + tk, :], acc_scr[r2:r2 + tk, :]
        o = a1[:, :V_HEAD_DIM] / a1[:, V_HEAD_DIM:] - lam * (a2[:, :V_HEAD_DIM] / a2[:, V_HEAD_DIM:])
        y = o * lax.rsqrt(jnp.mean(o * o, axis=-1, keepdims=True) + LN_EPS)
        o_ref[half * tk:(half + 1) * tk, :] = (y * g_ref[...] * (1.0 - LAM_INIT)).astype(BF16)


def _diff_attn(mix, slopes, lam_vecs, attn_g, batch, seq):
    m = mix.shape[0]
    nq = seq // ATT_TQ
    qc, kc, vc = COL_Q // V_HEAD_DIM, COL_K // V_HEAD_DIM, COL_VAL // V_HEAD_DIM
    return pl.pallas_call(
        _attn_kernel,
        grid=(batch, N_HEADS, nq),
        in_specs=[
            pl.BlockSpec(memory_space=pltpu.SMEM),
            pl.BlockSpec((4, HEAD_DIM), lambda b, h, i: (0, 0)),
            pl.BlockSpec((ATT_TQ, V_HEAD_DIM), lambda b, h, i: (b * nq + i, qc + h)),
            pl.BlockSpec((seq, V_HEAD_DIM), lambda b, h, i: (b, kc + h)),
            pl.BlockSpec((seq, V_HEAD_DIM), lambda b, h, i: (b, vc + h)),
            pl.BlockSpec((1, V_HEAD_DIM), lambda b, h, i: (0, h)),
        ],
        out_specs=pl.BlockSpec((ATT_TQ, V_HEAD_DIM), lambda b, h, i: (b * nq + i, h)),
        out_shape=jax.ShapeDtypeStruct((m, ATTN_WIDTH), BF16),
        scratch_shapes=[
            pltpu.VMEM((seq, V_HEAD_DIM + AUG), BF16),
            pltpu.VMEM((seq, V_HEAD_DIM + AUG), BF16),
            pltpu.VMEM((ATT_TK, ATT_TK), F32),
            pltpu.VMEM((ATT_ROWS, V_HEAD_DIM + AUG), BF16),
            pltpu.VMEM((ATT_ROWS, ATT_TK), F32),
            pltpu.VMEM((ATT_ROWS, ATT_TK), F32),
            pltpu.VMEM((ATT_ROWS, 128), F32),
            pltpu.VMEM((ATT_ROWS, V_HEAD_DIM + AUG), F32),
        ],
        compiler_params=_params(("parallel", "parallel", "arbitrary")),
        name="diff_attn",
    )(slopes, lam_vecs, mix, mix, mix, attn_g)


MERGE_TM = 512
MERGE_TC = 512
MERGE_EDGE_SPLIT = 2


def _merge_kernel(xf_ref, ya_ref, yb_ref, wga_ref, wgb_ref, wa_ref, wb_ref, wo_ref, g_ref, b_ref,
                  o_ref, xb_scr, acc_scr):
    j = pl.program_id(1)
    last = pl.num_programs(1) - 1

    def partial(rows):
        xb = xb_scr[rows, :]
        merged = (jax.nn.sigmoid(_dot(xb, wga_ref[...])) * _dot(ya_ref[rows, :], wa_ref[...])
                  + jax.nn.sigmoid(_dot(xb, wgb_ref[...])) * _dot(yb_ref[rows, :], wb_ref[...]))
        return _dot(merged.astype(BF16), wo_ref[...])

    @pl.when(j == 0)
    def _():
        for rows in _row_parts(MERGE_TM, MERGE_EDGE_SPLIT):
            xb_scr[rows, :] = xf_ref[rows, :].astype(BF16)
            acc_scr[rows, :] = partial(rows)

    @pl.when((j > 0) & (j < last))
    def _():
        acc_scr[...] += partial(slice(None))

    @pl.when(j == last)
    def _():
        for rows in _row_parts(MERGE_TM, MERGE_EDGE_SPLIT):
            y = ALPHA * xf_ref[rows, :] + (acc_scr[rows, :] + partial(rows))
            o_ref[rows, :] = _layer_norm(y, g_ref[...], b_ref[...])


def _merge_out(xf, ya, yb, w_in, w_a, w_b, w_o, g, b):
    m, d = xf.shape
    nc = d // MERGE_TC
    ga0, gb0 = COL_GA // MERGE_TC, COL_GB // MERGE_TC
    row = lambda i, j: (i, 0)
    return pl.pallas_call(
        _merge_kernel,
        grid=(m // MERGE_TM, nc),
        in_specs=[
            pl.BlockSpec((MERGE_TM, d), row),
            pl.BlockSpec((MERGE_TM, SGU_WIDTH), row),
            pl.BlockSpec((MERGE_TM, ATTN_WIDTH), row),
            pl.BlockSpec((d, MERGE_TC), lambda i, j: (0, ga0 + j)),
            pl.BlockSpec((d, MERGE_TC), lambda i, j: (0, gb0 + j)),
            pl.BlockSpec((SGU_WIDTH, MERGE_TC), lambda i, j: (0, j)),
            pl.BlockSpec((ATTN_WIDTH, MERGE_TC), lambda i, j: (0, j)),
            pl.BlockSpec((MERGE_TC, d), lambda i, j: (j, 0)),
            pl.BlockSpec((1, d), lambda i, j: (0, 0)),
            pl.BlockSpec((1, d), lambda i, j: (0, 0)),
        ],
        out_specs=pl.BlockSpec((MERGE_TM, d), row),
        out_shape=jax.ShapeDtypeStruct((m, d), F32),
        scratch_shapes=[pltpu.VMEM((MERGE_TM, d), BF16), pltpu.VMEM((MERGE_TM, d), F32)],
        compiler_params=_params(("parallel", "arbitrary")),
        name="merge_out",
    )(xf, ya, yb, w_in, w_in, w_a, w_b, w_o, g, b)


PE_TM = 512
PE_SPLIT = 2


def _pe_kernel(xf_ref, p_ref, wg_ref, wp_ref, g_ref, b_ref, o_ref):
    for rows in _row_parts(PE_TM, PE_SPLIT):
        x = xf_ref[rows, :]
        gate = jax.nn.sigmoid(_dot(x.astype(BF16), wg_ref[...]))
        emb = _dot(p_ref[rows, :].astype(BF16), wp_ref[...])
        o_ref[rows, :] = _layer_norm(ALPHA * x + gate * emb, g_ref[...], b_ref[...])


def _pe_gate(xf, p, w_g, w_p, g, b):
    m, d = xf.shape
    row = lambda i: (i, 0)
    whole = lambda i: (0, 0)
    return pl.pallas_call(
        _pe_kernel,
        grid=(m // PE_TM,),
        in_specs=[
            pl.BlockSpec((PE_TM, d), row),
            pl.BlockSpec((PE_TM, P_DIM), row),
            pl.BlockSpec((d, d), whole),
            pl.BlockSpec((P_DIM, d), whole),
            pl.BlockSpec((1, d), whole),
            pl.BlockSpec((1, d), whole),
        ],
        out_specs=pl.BlockSpec((PE_TM, d), row),
        out_shape=jax.ShapeDtypeStruct((m, d), F32),
        compiler_params=_params(("parallel",)),
        name="pe_gate",
    )(xf, p, w_g, w_p, g, b)


def kernel(x, p, ffn1_w_gu, ffn1_w_down, ln1_g, ln1_b, w_in, sgu_ln_g, sgu_ln_b, sgu_w, sgu_b, lam_q1, lam_k1, lam_q2, lam_k2, attn_norm_g, w_branch_a, w_branch_b, w_out, ln2_g, ln2_b, ffn2_w_gu, ffn2_w_down, ln3_g, ln3_b, w_pe_gate, w_pe_proj, ln4_g, ln4_b):
    batch, seq, d = x.shape
    m = batch * seq
    slopes = jnp.asarray(2.0 ** (-8.0 * np.arange(1, N_HEADS + 1) / N_HEADS), dtype=F32)
    xf = x.reshape(m, d)
    for i in range(DEPTH):
        w_in_b = w_in[i].astype(BF16)
        bias_full = jnp.repeat(sgu_b[i].T, SGU_BLOCK, axis=1)
        lam_vecs = jnp.stack([lam_q1[i], lam_k1[i], lam_q2[i], lam_k2[i]]).astype(F32)

        x1 = _ffn_ln(xf, ffn1_w_gu[i].astype(BF16), ffn1_w_down[i], ln1_g[i][None], ln1_b[i][None])
        mix = _in_proj(x1, w_in_b, sgu_ln_g[i][None], sgu_ln_b[i][None])
        y_a = _sgu(mix, sgu_w[i], bias_full)
        y_b = _diff_attn(mix, slopes, lam_vecs, attn_norm_g[i][None], batch, seq)
        x2 = _merge_out(x1, y_a, y_b, w_in_b, w_branch_a[i].astype(BF16), w_branch_b[i].astype(BF16),
                        w_out[i].astype(BF16), ln2_g[i][None], ln2_b[i][None])
        x3 = _ffn_ln(x2, ffn2_w_gu[i].astype(BF16), ffn2_w_down[i], ln3_g[i][None], ln3_b[i][None])
        xf = _pe_gate(x3, p[i].reshape(m, P_DIM), w_pe_gate[i].astype(BF16), w_pe_proj[i].astype(BF16),
                      ln4_g[i][None], ln4_b[i][None])
    return xf.reshape(batch, seq, d)
```

```python
import math

import jax
import jax.numpy as jnp
import numpy as np
from jax import lax
from jax.experimental import pallas as pl
from jax.experimental.pallas import tpu as pltpu

CHUNK = 64
P_DIM = 256
SGU_BLOCK = 128
SGU_GROUPS = 8
SGU_WIDTH = 1024
N_HEADS = 8
HEAD_DIM = 64
V_HEAD_DIM = 128
ATTN_WIDTH = 1024
DEPTH = 1
ALPHA = (2 * DEPTH) ** 0.25
LN_EPS = 1e-5
LAM_INIT = 0.8 - 0.6 * math.exp(-0.3 * 0)

COL_U, COL_V, COL_Q, COL_K, COL_VAL = 0, 1024, 2048, 3072, 4096
COL_GA = 5120
MIX_COLS = 5120

BF16 = jnp.bfloat16
F32 = jnp.float32

VMEM_LIMIT_BYTES = 60 * 1024 * 1024

MASK_VALUE = -1e30
BF16_SUBLANES = 16
GATE_BLOCK = 1024


def _params(semantics):
    return pltpu.CompilerParams(dimension_semantics=semantics, vmem_limit_bytes=VMEM_LIMIT_BYTES)


def _dot(a, b):
    return jnp.dot(a, b, preferred_element_type=F32)


def _layer_norm(y, g, b):
    mu = jnp.mean(y, axis=-1, keepdims=True)
    d = y - mu
    var = jnp.mean(d * d, axis=-1, keepdims=True)
    return d * lax.rsqrt(var + LN_EPS) * g + b


FFN_TM = 1024
FFN_TF = 512
FFN_EDGE_SPLIT = 4


def _row_parts(n_rows, parts):
    size = n_rows // parts
    return [slice(k * size, (k + 1) * size) for k in range(parts)]


def _ffn_ln_kernel(x_ref, wg_ref, wu_ref, wd_ref, g_ref, b_ref, o_ref, xb_scr):
    j = pl.program_id(1)
    last = pl.num_programs(1) - 1

    def partial(rows, wd):
        xb = xb_scr[rows, :]
        gate = _dot(xb, wg_ref[...])
        up = _dot(xb, wu_ref[...])
        h = (gate * jax.nn.sigmoid(gate)) * up
        return _dot(h.astype(BF16), wd)

    def wd_chunk():
        return wd_ref[...].astype(BF16)

    @pl.when(j == 0)
    def _():
        wd = wd_chunk()
        for rows in _row_parts(FFN_TM, FFN_EDGE_SPLIT):
            xb_scr[rows, :] = x_ref[rows, :].astype(BF16)
            o_ref[rows, :] = partial(rows, wd)

    @pl.when((j > 0) & (j < last))
    def _():
        o_ref[...] += partial(slice(None), wd_chunk())

    @pl.when(j == last)
    def _():
        wd = wd_chunk()
        for rows in _row_parts(FFN_TM, FFN_EDGE_SPLIT):
            y = ALPHA * x_ref[rows, :] + 0.5 * (o_ref[rows, :] + partial(rows, wd))
            o_ref[rows, :] = _layer_norm(y, g_ref[...], b_ref[...])


def _ffn_ln(x, w_gu, w_down, g, b):
    m, d = x.shape
    f = w_down.shape[0]
    nf = f // FFN_TF
    return pl.pallas_call(
        _ffn_ln_kernel,
        grid=(m // FFN_TM, nf),
        in_specs=[
            pl.BlockSpec((FFN_TM, d), lambda i, j: (i, 0)),
            pl.BlockSpec((d, FFN_TF), lambda i, j: (0, j)),
            pl.BlockSpec((d, FFN_TF), lambda i, j: (0, nf + j)),
            pl.BlockSpec((FFN_TF, d), lambda i, j: (j, 0)),
            pl.BlockSpec((1, d), lambda i, j: (0, 0)),
            pl.BlockSpec((1, d), lambda i, j: (0, 0)),
        ],
        out_specs=pl.BlockSpec((FFN_TM, d), lambda i, j: (i, 0)),
        out_shape=jax.ShapeDtypeStruct((m, d), F32),
        scratch_shapes=[pltpu.VMEM((FFN_TM, d), BF16)],
        compiler_params=_params(("parallel", "arbitrary")),
        name="ffn_ln",
    )(x, w_gu, w_gu, w_down, g, b)


PROJ_TM = 1024
PROJ_TN = 1024
PROJ_EDGE_SPLIT = 2
PROJ_NORM_SPLIT = 4


def _in_proj_kernel(x_ref, w_ref, g_ref, b_ref, o_ref, xb_scr):
    j = pl.program_id(1)

    def proj(rows=slice(None)):
        return _dot(xb_scr[rows, :], w_ref[...])

    @pl.when(j == COL_U // PROJ_TN)
    def _():
        for rows in _row_parts(PROJ_TM, PROJ_EDGE_SPLIT):
            xb_scr[rows, :] = x_ref[rows, :].astype(BF16)
            o_ref[rows, :] = jax.nn.gelu(proj(rows)).astype(BF16)

    @pl.when(j == COL_V // PROJ_TN)
    def _():
        for rows in _row_parts(PROJ_TM, PROJ_NORM_SPLIT):
            o_ref[rows, :] = _layer_norm(jax.nn.gelu(proj(rows)), g_ref[...], b_ref[...]).astype(BF16)

    @pl.when(j == COL_Q // PROJ_TN)
    def _():
        o_ref[...] = (proj() * HEAD_DIM ** -0.5).astype(BF16)

    @pl.when(j >= COL_K // PROJ_TN)
    def _():
        o_ref[...] = proj().astype(BF16)


def _in_proj(x, w_in, sgu_g, sgu_b):
    m, d = x.shape
    return pl.pallas_call(
        _in_proj_kernel,
        grid=(m // PROJ_TM, MIX_COLS // PROJ_TN),
        in_specs=[
            pl.BlockSpec((PROJ_TM, d), lambda i, j: (i, 0)),
            pl.BlockSpec((d, PROJ_TN), lambda i, j: (0, j)),
            pl.BlockSpec((1, SGU_WIDTH), lambda i, j: (0, 0)),
            pl.BlockSpec((1, SGU_WIDTH), lambda i, j: (0, 0)),
        ],
        out_specs=pl.BlockSpec((PROJ_TM, PROJ_TN), lambda i, j: (i, j)),
        out_shape=jax.ShapeDtypeStruct((m, MIX_COLS), BF16),
        scratch_shapes=[pltpu.VMEM((PROJ_TM, d), BF16)],
        compiler_params=_params(("parallel", "arbitrary")),
        name="in_proj",
    )(x, w_in, sgu_g, sgu_b)


SGU_TM = 512


def _sgu_kernel(u_ref, v_ref, w_ref, bias_ref, o_ref):
    t = lax.broadcasted_iota(jnp.int32, (SGU_BLOCK, SGU_BLOCK), 0)
    s = lax.broadcasted_iota(jnp.int32, (SGU_BLOCK, SGU_BLOCK), 1)
    allowed = (s // CHUNK) <= (t // CHUNK)
    for g in range(SGU_GROUPS):
        wm = jnp.where(allowed, w_ref[g], 0.0).astype(BF16)
        cols = slice(g * SGU_BLOCK, (g + 1) * SGU_BLOCK)
        bias = bias_ref[:, cols]
        for blk in range(SGU_TM // SGU_BLOCK):
            rows = slice(blk * SGU_BLOCK, (blk + 1) * SGU_BLOCK)
            mix = _dot(wm, v_ref[rows, cols]) + bias
            o_ref[rows, cols] = (u_ref[rows, cols].astype(F32) * mix).astype(BF16)


def _sgu(mix, sgu_w, bias_full):
    m = mix.shape[0]
    return pl.pallas_call(
        _sgu_kernel,
        grid=(m // SGU_TM,),
        in_specs=[
            pl.BlockSpec((SGU_TM, SGU_WIDTH), lambda i: (i, COL_U // SGU_WIDTH)),
            pl.BlockSpec((SGU_TM, SGU_WIDTH), lambda i: (i, COL_V // SGU_WIDTH)),
            pl.BlockSpec((SGU_GROUPS, SGU_BLOCK, SGU_BLOCK), lambda i: (0, 0, 0)),
            pl.BlockSpec((SGU_BLOCK, SGU_WIDTH), lambda i: (0, 0)),
        ],
        out_specs=pl.BlockSpec((SGU_TM, SGU_WIDTH), lambda i: (i, 0)),
        out_shape=jax.ShapeDtypeStruct((m, SGU_WIDTH), BF16),
        compiler_params=_params(("parallel",)),
        name="sgu",
    )(mix, mix, sgu_w, bias_full)


ATT_TK = 512
ATT_TQ = 2 * ATT_TK
ATT_ROWS = 2 * ATT_TQ
AUG = 128


def _lane_tile(x, n):
    return jnp.concatenate([x] * n, axis=1)


N_SIDE_PLAIN = 5
N_GATE_BLOCKS = 4


def _attn_kernel(slopes_ref, lam_ref, q_ref, k_ref, v_ref, g_ref, *refs):
    side_in = refs[:N_SIDE_PLAIN]
    gate_in = refs[N_SIDE_PLAIN:N_SIDE_PLAIN + N_GATE_BLOCKS]
    o_ref = refs[N_SIDE_PLAIN + N_GATE_BLOCKS]
    side_out = refs[N_SIDE_PLAIN + N_GATE_BLOCKS + 1:2 * N_SIDE_PLAIN + N_GATE_BLOCKS + 1]
    gate_out = refs[2 * N_SIDE_PLAIN + N_GATE_BLOCKS + 1]
    kaug_scr, vaug_scr, diag_scr, qa_scr, sa_scr, sb_scr, m_scr, acc_scr = refs[2 * N_SIDE_PLAIN + N_GATE_BLOCKS + 2:]
    _attn_body(slopes_ref, lam_ref, q_ref, k_ref, v_ref, g_ref, side_in, gate_in, o_ref, side_out, gate_out,
               kaug_scr, vaug_scr, diag_scr, qa_scr, sa_scr, sb_scr, m_scr, acc_scr)


def _attn_body(slopes_ref, lam_ref, q_ref, k_ref, v_ref, g_ref, side_in, gate_in, o_ref, side_out, gate_out,
               kaug_scr, vaug_scr, diag_scr, qa_scr, sa_scr, sb_scr, m_scr, acc_scr):
    h = pl.program_id(1)
    qi = pl.program_id(2)
    tq, tk, rows = ATT_TQ, ATT_TK, ATT_ROWS
    seq = k_ref.shape[0]
    slope = slopes_ref[h]
    q0 = pl.multiple_of(qi * tq, tq)

    @pl.when(qi == 0)
    def _():
        pos = lax.broadcasted_iota(jnp.int32, (seq, AUG), 0)
        lane = lax.broadcasted_iota(jnp.int32, (seq, AUG), 1)
        feat = jnp.where(lane == 0, pos // CHUNK, jnp.where(lane == 1, pos % CHUNK, jnp.where(lane == 2, 1, 0)))
        kaug_scr[:, :V_HEAD_DIM] = k_ref[...]
        kaug_scr[:, V_HEAD_DIM:] = feat.astype(F32).astype(BF16)
        vaug_scr[:, :V_HEAD_DIM] = v_ref[...]
        vaug_scr[:, V_HEAD_DIM:] = jnp.ones((seq, AUG), BF16)
        t = lax.broadcasted_iota(jnp.int32, (tk, tk), 0)
        c = lax.broadcasted_iota(jnp.int32, (tk, tk), 1)
        ahead = jnp.maximum(c - t, 0).astype(F32)
        diag_scr[...] = jnp.where(c // CHUNK <= t // CHUNK, (-2.0 * slope) * ahead, MASK_VALUE)

    lane = lax.broadcasted_iota(jnp.int32, (tk, V_HEAD_DIM), 1)
    aug = jnp.where(lane == 0, CHUNK * slope,
                    jnp.where(lane == 1, slope, jnp.where(lane == 2, -slope * q0.astype(F32), 0.0))).astype(BF16)
    for half in range(2):
        q = q_ref[half * tk:(half + 1) * tk, :]
        zero = jnp.zeros_like(q)
        r1, r2 = 2 * half * tk, (2 * half + 1) * tk
        qa_scr[r1:r1 + tk, :V_HEAD_DIM] = jnp.where(lane < HEAD_DIM, q, zero)
        qa_scr[r2:r2 + tk, :V_HEAD_DIM] = jnp.where(lane >= HEAD_DIM, q, zero)
        qa_scr[r1:r1 + tk, V_HEAD_DIM:] = aug
        qa_scr[r2:r2 + tk, V_HEAD_DIM:] = aug
    m_scr[...] = jnp.full_like(m_scr, MASK_VALUE)
    acc_scr[...] = jnp.zeros_like(acc_scr)

    def scores(k0, r0=0):
        return lax.dot_general(qa_scr[r0:, :], kaug_scr[pl.ds(k0, tk), :], (((1,), (1,)), ((), ())),
                               preferred_element_type=F32)

    def update(s, k0, rs=slice(None)):
        width = s.shape[1]
        m_prev = m_scr[rs, :]
        m_new = jnp.maximum(m_prev, jnp.max(s, axis=1, keepdims=True))
        alpha = jnp.exp(m_prev - m_new)
        p = jnp.exp(s - _lane_tile(m_new, width // 128))
        m_scr[rs, :] = m_new
        acc_scr[rs, :] = (acc_scr[rs, :] * _lane_tile(alpha, 2)
                          + _dot(p.astype(BF16), vaug_scr[pl.ds(k0, width), :]))

    for w_ref, wb_ref in zip(side_in, side_out):
        wb_ref[...] = w_ref[...].astype(BF16)
    for k, w_ref in enumerate(gate_in):
        gate_out[:, k * GATE_BLOCK:(k + 1) * GATE_BLOCK] = w_ref[...].astype(BF16)
    sa_scr[...] = scores(0)

    def body(pair, carry):
        k0 = pl.multiple_of(pair * (2 * tk), 2 * tk)
        sb_scr[...] = scores(k0 + tk)
        update(sa_scr[...], k0)
        sa_scr[...] = scores(k0 + 2 * tk)
        update(sb_scr[...], k0 + tk)
        return carry

    lax.fori_loop(0, qi, body, 0)

    first, second = slice(0, 2 * tk), slice(2 * tk, rows)
    diag2 = jnp.concatenate([diag_scr[...], diag_scr[...]], axis=0)
    sb_scr[second, :] = scores(q0 + tk, 2 * tk)
    update(sa_scr[first, :] + diag2, q0, first)
    update(jnp.concatenate([sa_scr[second, :], sb_scr[second, :] + diag2], axis=1), q0, second)

    lv = lam_ref[...]
    lam = (jnp.exp(jnp.sum(lv[0:1] * lv[1:2], axis=-1, keepdims=True))
           - jnp.exp(jnp.sum(lv[2:3] * lv[3:4], axis=-1, keepdims=True)) + LAM_INIT)
    for half in range(2):
        r1, r2 = 2 * half * tk, (2 * half + 1) * tk
        a1, a2 = acc_scr[r1:r1 + tk, :], acc_scr[r2:r2 + tk, :]
        o = a1[:, :V_HEAD_DIM] / a1[:, V_HEAD_DIM:] - lam * (a2[:, :V_HEAD_DIM] / a2[:, V_HEAD_DIM:])
        y = o * lax.rsqrt(jnp.mean(o * o, axis=-1, keepdims=True) + LN_EPS)
        o_ref[half * tk:(half + 1) * tk, :] = (y * g_ref[...] * (1.0 - LAM_INIT)).astype(BF16)


def _diff_attn(mix, slopes, lam_vecs, attn_g, batch, seq, cast_ws, w_in):
    m = mix.shape[0]
    nq = seq // ATT_TQ
    qc, kc, vc = COL_Q // V_HEAD_DIM, COL_K // V_HEAD_DIM, COL_VAL // V_HEAD_DIM
    steps = batch * N_HEADS * nq
    slab = lambda b, h, i: ((b * N_HEADS + h) * nq + i, 0)
    assert len(cast_ws) == N_SIDE_PLAIN

    def slab_rows(w):
        rows = w.shape[0] // steps
        assert rows * steps == w.shape[0] and rows % BF16_SUBLANES == 0, w.shape
        return rows

    slab_specs = [pl.BlockSpec((slab_rows(w), w.shape[1]), slab) for w in cast_ws]
    gate_rows = slab_rows(w_in)
    gate_in_specs = [
        pl.BlockSpec((gate_rows, GATE_BLOCK), lambda b, h, i, k=k: (slab(b, h, i)[0], COL_GA // GATE_BLOCK + k))
        for k in range(N_GATE_BLOCKS)]
    gate_out_spec = pl.BlockSpec((gate_rows, N_GATE_BLOCKS * GATE_BLOCK), slab)
    outs = pl.pallas_call(
        _attn_kernel,
        grid=(batch, N_HEADS, nq),
        in_specs=[
            pl.BlockSpec(memory_space=pltpu.SMEM),
            pl.BlockSpec((4, HEAD_DIM), lambda b, h, i: (0, 0)),
            pl.BlockSpec((ATT_TQ, V_HEAD_DIM), lambda b, h, i: (b * nq + i, qc + h)),
            pl.BlockSpec((seq, V_HEAD_DIM), lambda b, h, i: (b, kc + h)),
            pl.BlockSpec((seq, V_HEAD_DIM), lambda b, h, i: (b, vc + h)),
            pl.BlockSpec((1, V_HEAD_DIM), lambda b, h, i: (0, h)),
        ] + slab_specs + gate_in_specs,
        out_specs=[pl.BlockSpec((ATT_TQ, V_HEAD_DIM), lambda b, h, i: (b * nq + i, h))] + slab_specs
        + [gate_out_spec],
        out_shape=[jax.ShapeDtypeStruct((m, ATTN_WIDTH), BF16)]
        + [jax.ShapeDtypeStruct(w.shape, BF16) for w in cast_ws]
        + [jax.ShapeDtypeStruct((w_in.shape[0], N_GATE_BLOCKS * GATE_BLOCK), BF16)],
        scratch_shapes=[
            pltpu.VMEM((seq, V_HEAD_DIM + AUG), BF16),
            pltpu.VMEM((seq, V_HEAD_DIM + AUG), BF16),
            pltpu.VMEM((ATT_TK, ATT_TK), F32),
            pltpu.VMEM((ATT_ROWS, V_HEAD_DIM + AUG), BF16),
            pltpu.VMEM((ATT_ROWS, ATT_TK), F32),
            pltpu.VMEM((ATT_ROWS, ATT_TK), F32),
            pltpu.VMEM((ATT_ROWS, 128), F32),
            pltpu.VMEM((ATT_ROWS, V_HEAD_DIM + AUG), F32),
        ],
        compiler_params=_params(("parallel", "parallel", "arbitrary")),
        name="diff_attn",
    )(slopes, lam_vecs, mix, mix, mix, attn_g, *cast_ws, *([w_in] * N_GATE_BLOCKS))
    return outs[0], outs[1:1 + N_SIDE_PLAIN], outs[1 + N_SIDE_PLAIN]


MERGE_TM = 512
MERGE_TC = 512
MERGE_EDGE_SPLIT = 2


def _merge_kernel(xf_ref, ya_ref, yb_ref, wga_ref, wgb_ref, wa_ref, wb_ref, wo_ref, g_ref, b_ref,
                  o_ref, xb_scr, acc_scr):
    j = pl.program_id(1)
    last = pl.num_programs(1) - 1

    def partial(rows):
        xb = xb_scr[rows, :]
        merged = (jax.nn.sigmoid(_dot(xb, wga_ref[...])) * _dot(ya_ref[rows, :], wa_ref[...])
                  + jax.nn.sigmoid(_dot(xb, wgb_ref[...])) * _dot(yb_ref[rows, :], wb_ref[...]))
        return _dot(merged.astype(BF16), wo_ref[...])

    @pl.when(j == 0)
    def _():
        for rows in _row_parts(MERGE_TM, MERGE_EDGE_SPLIT):
            xb_scr[rows, :] = xf_ref[rows, :].astype(BF16)
            acc_scr[rows, :] = partial(rows)

    @pl.when((j > 0) & (j < last))
    def _():
        acc_scr[...] += partial(slice(None))

    @pl.when(j == last)
    def _():
        for rows in _row_parts(MERGE_TM, MERGE_EDGE_SPLIT):
            y = ALPHA * xf_ref[rows, :] + (acc_scr[rows, :] + partial(rows))
            o_ref[rows, :] = _layer_norm(y, g_ref[...], b_ref[...])


def _merge_out(xf, ya, yb, w_gates, w_a, w_b, w_o, g, b):
    m, d = xf.shape
    nc = d // MERGE_TC
    ga0, gb0 = 0, d // MERGE_TC
    row = lambda i, j: (i, 0)
    return pl.pallas_call(
        _merge_kernel,
        grid=(m // MERGE_TM, nc),
        in_specs=[
            pl.BlockSpec((MERGE_TM, d), row),
            pl.BlockSpec((MERGE_TM, SGU_WIDTH), row),
            pl.BlockSpec((MERGE_TM, ATTN_WIDTH), row),
            pl.BlockSpec((d, MERGE_TC), lambda i, j: (0, ga0 + j)),
            pl.BlockSpec((d, MERGE_TC), lambda i, j: (0, gb0 + j)),
            pl.BlockSpec((SGU_WIDTH, MERGE_TC), lambda i, j: (0, j)),
            pl.BlockSpec((ATTN_WIDTH, MERGE_TC), lambda i, j: (0, j)),
            pl.BlockSpec((MERGE_TC, d), lambda i, j: (j, 0)),
            pl.BlockSpec((1, d), lambda i, j: (0, 0)),
            pl.BlockSpec((1, d), lambda i, j: (0, 0)),
        ],
        out_specs=pl.BlockSpec((MERGE_TM, d), row),
        out_shape=jax.ShapeDtypeStruct((m, d), F32),
        scratch_shapes=[pltpu.VMEM((MERGE_TM, d), BF16), pltpu.VMEM((MERGE_TM, d), F32)],
        compiler_params=_params(("parallel", "arbitrary")),
        name="merge_out",
    )(xf, ya, yb, w_gates, w_gates, w_a, w_b, w_o, g, b)


PE_TM = 512
PE_SPLIT = 2


def _pe_kernel(xf_ref, p_ref, wg_ref, wp_ref, g_ref, b_ref, o_ref):
    for rows in _row_parts(PE_TM, PE_SPLIT):
        x = xf_ref[rows, :]
        gate = jax.nn.sigmoid(_dot(x.astype(BF16), wg_ref[...]))
        emb = _dot(p_ref[rows, :].astype(BF16), wp_ref[...])
        o_ref[rows, :] = _layer_norm(ALPHA * x + gate * emb, g_ref[...], b_ref[...])


def _pe_gate(xf, p, w_g, w_p, g, b):
    m, d = xf.shape
    row = lambda i: (i, 0)
    whole = lambda i: (0, 0)
    return pl.pallas_call(
        _pe_kernel,
        grid=(m // PE_TM,),
        in_specs=[
            pl.BlockSpec((PE_TM, d), row),
            pl.BlockSpec((PE_TM, P_DIM), row),
            pl.BlockSpec((d, d), whole),
            pl.BlockSpec((P_DIM, d), whole),
            pl.BlockSpec((1, d), whole),
            pl.BlockSpec((1, d), whole),
        ],
        out_specs=pl.BlockSpec((PE_TM, d), row),
        out_shape=jax.ShapeDtypeStruct((m, d), F32),
        compiler_params=_params(("parallel",)),
        name="pe_gate",
    )(xf, p, w_g, w_p, g, b)


def kernel(x, p, ffn1_w_gu, ffn1_w_down, ln1_g, ln1_b, w_in, sgu_ln_g, sgu_ln_b, sgu_w, sgu_b, lam_q1, lam_k1, lam_q2, lam_k2, attn_norm_g, w_branch_a, w_branch_b, w_out, ln2_g, ln2_b, ffn2_w_gu, ffn2_w_down, ln3_g, ln3_b, w_pe_gate, w_pe_proj, ln4_g, ln4_b):
    batch, seq, d = x.shape
    m = batch * seq
    slopes = jnp.asarray(2.0 ** (-8.0 * np.arange(1, N_HEADS + 1) / N_HEADS), dtype=F32)
    xf = x.reshape(m, d)
    for i in range(DEPTH):
        w_mix = w_in[i][:, :MIX_COLS].astype(BF16)
        bias_full = jnp.repeat(sgu_b[i].T, SGU_BLOCK, axis=1)
        lam_vecs = jnp.stack([lam_q1[i], lam_k1[i], lam_q2[i], lam_k2[i]]).astype(F32)

        x1 = _ffn_ln(xf, ffn1_w_gu[i].astype(BF16), ffn1_w_down[i], ln1_g[i][None], ln1_b[i][None])
        mix = _in_proj(x1, w_mix, sgu_ln_g[i][None], sgu_ln_b[i][None])
        y_a = _sgu(mix, sgu_w[i], bias_full)
        wa_shape = w_branch_a[i].shape
        slab_view = (d, wa_shape[0] * wa_shape[1] // d)
        y_b, (w_gu2, w_o, w_pg, w_a, w_b), w_gates = _diff_attn(
            mix, slopes, lam_vecs, attn_norm_g[i][None], batch, seq,
            (ffn2_w_gu[i], w_out[i], w_pe_gate[i], w_branch_a[i].reshape(slab_view),
             w_branch_b[i].reshape(slab_view)), w_in[i])
        x2 = _merge_out(x1, y_a, y_b, w_gates, w_a.reshape(wa_shape), w_b.reshape(wa_shape),
                        w_o, ln2_g[i][None], ln2_b[i][None])
        x3 = _ffn_ln(x2, w_gu2, ffn2_w_down[i], ln3_g[i][None], ln3_b[i][None])
        xf = _pe_gate(x3, p[i].reshape(m, P_DIM), w_pg, w_pe_proj[i].astype(BF16),
                      ln4_g[i][None], ln4_b[i][None])
    return xf.reshape(batch, seq, d)
```
